```python
import math
import jax, jax.numpy as jnp
from jax import lax
import numpy as np

D_MODEL = 2048
BATCH = 4
SEQ = 8192
DEPTH = 4

D_MIX = D_MODEL
ATTN_WIDTH = D_MIX // 2
POOL_WIDTH = D_MIX - ATTN_WIDTH
N_HEADS = 8
D_V = ATTN_WIDTH // N_HEADS
D_QK = D_V // 2
Q_COLS = N_HEADS * 2 * D_QK
POOL_WINDOWS = (2, 4, 8, 16)
N_POOL_GROUPS = len(POOL_WINDOWS)
POOL_GROUP = POOL_WIDTH // N_POOL_GROUPS
IN_COLS = 2 * Q_COLS + ATTN_WIDTH + POOL_WIDTH
D_FF = 5632
CONV_WIDTH = 3
Q_BLOCK = 128
EPS = 1e-6
LAMBDA_STD = 0.1

kernel_name = "hybrid_diffattn_multipool_sandwich"


def rms_norm(x, g):
    xf = x.astype(jnp.float32)
    y = xf * lax.rsqrt(jnp.mean(xf * xf, axis=-1, keepdims=True) + EPS)
    return (y * g.astype(jnp.float32)).astype(x.dtype)


def alibi_slopes(n_heads):
    return 2.0 ** (-8.0 * jnp.arange(1, n_heads + 1, dtype=jnp.float32) / n_heads)


def diff_attention(q, k, v, lam):
    B, S, H, _, Dk = q.shape
    nb = S // Q_BLOCK
    scale = Dk ** -0.5
    slopes = alibi_slopes(H)
    kpos = jnp.arange(S)
    q_blocks = q.reshape(B, nb, Q_BLOCK, H, 2, Dk).transpose(1, 0, 2, 3, 4, 5)

    def one_block(args):
        q_blk, blk = args
        qpos = blk * Q_BLOCK + jnp.arange(Q_BLOCK)
        dist = qpos[:, None] - kpos[None, :]
        bias = -slopes[:, None, None] * dist.astype(jnp.float32)[None]
        bias = jnp.where((dist >= 0)[None], bias, -jnp.inf)
        logits = jnp.einsum('bqhmd,bkhmd->bhmqk', q_blk, k).astype(jnp.float32) * scale
        a = jax.nn.softmax(logits + bias[None, :, None], axis=-1)
        p = a[:, :, 0] - lam * a[:, :, 1]
        return jnp.einsum('bhqk,bkhe->bqhe', p.astype(v.dtype), v)

    o = lax.map(one_block, (q_blocks, jnp.arange(nb)))
    return o.transpose(1, 0, 2, 3, 4).reshape(B, S, H, v.shape[-1])


def multiscale_pool(u):
    B, S, _ = u.shape
    ug = u.reshape(B, S, N_POOL_GROUPS, POOL_GROUP).astype(jnp.float32)
    cs = jnp.cumsum(ug, axis=1)
    t = jnp.arange(1, S + 1, dtype=jnp.float32)
    outs = []
    for g, w in enumerate(POOL_WINDOWS):
        c = cs[:, :, g]
        prev = jnp.pad(c, ((0, 0), (w, 0), (0, 0)))[:, :S]
        cnt = jnp.minimum(t, float(w))[None, :, None]
        outs.append((c - prev) / cnt - ug[:, :, g])
    return jnp.stack(outs, axis=2)


def causal_dwconv(h, w, b):
    K = w.shape[0]
    S = h.shape[1]
    hp = jnp.pad(h, ((0, 0), (K - 1, 0), (0, 0)))
    y = b
    for j in range(K):
        y = y + hp[:, j:j + S] * w[j]
    return y


def setup_inputs(seed: int = 0) -> dict:
    key = jax.random.key(seed)
    ks = jax.random.split(key, 20)
    f32 = jnp.float32

    def nrm(k, shape, scale):
        return jax.random.normal(k, shape, f32) * scale

    def gain(k, n):
        return 1.0 + 0.02 * jax.random.normal(k, (DEPTH, n), f32)

    return {
        "x": jax.random.normal(ks[0], (BATCH, SEQ, D_MODEL), f32),
        "g_mix_pre": gain(ks[1], D_MODEL),
        "w_in": nrm(ks[2], (DEPTH, D_MODEL, IN_COLS), D_MODEL ** -0.5),
        "lam_q1": nrm(ks[3], (DEPTH, D_QK), LAMBDA_STD),
        "lam_k1": nrm(ks[4], (DEPTH, D_QK), LAMBDA_STD),
        "lam_q2": nrm(ks[5], (DEPTH, D_QK), LAMBDA_STD),
        "lam_k2": nrm(ks[6], (DEPTH, D_QK), LAMBDA_STD),
        "g_head": gain(ks[7], ATTN_WIDTH),
        "w_pool": nrm(ks[8], (DEPTH, N_POOL_GROUPS, POOL_GROUP, POOL_GROUP), POOL_GROUP ** -0.5),
        "pool_scale": gain(ks[9], POOL_WIDTH),
        "w_out": nrm(ks[10], (DEPTH, D_MIX, D_MODEL), D_MIX ** -0.5),
        "g_mix_post": gain(ks[11], D_MODEL),
        "g_ffn_pre": gain(ks[12], D_MODEL),
        "w_up": nrm(ks[13], (DEPTH, D_MODEL, 2 * D_FF), D_MODEL ** -0.5),
        "conv_w": nrm(ks[14], (DEPTH, CONV_WIDTH, 2 * D_FF), CONV_WIDTH ** -0.5),
        "conv_b": nrm(ks[15], (DEPTH, 2 * D_FF), 0.01),
        "w_down": nrm(ks[16], (DEPTH, D_FF, D_MODEL), D_FF ** -0.5),
        "g_ffn_post": gain(ks[17], D_MODEL),
    }


def reference(x, g_mix_pre, w_in, lam_q1, lam_k1, lam_q2, lam_k2, g_head, w_pool,
              pool_scale, w_out, g_mix_post, g_ffn_pre, w_up, conv_w, conv_b, w_down,
              g_ffn_post):
    B, S, _ = x.shape
    for i in range(DEPTH):
        h = rms_norm(x, g_mix_pre[i])
        proj = h @ w_in[i]
        q = proj[..., :Q_COLS].reshape(B, S, N_HEADS, 2, D_QK)
        k = proj[..., Q_COLS:2 * Q_COLS].reshape(B, S, N_HEADS, 2, D_QK)
        v = proj[..., 2 * Q_COLS:2 * Q_COLS + ATTN_WIDTH].reshape(B, S, N_HEADS, D_V)
        u = proj[..., 2 * Q_COLS + ATTN_WIDTH:]

        lam_init = 0.8 - 0.6 * math.exp(-0.3 * i)
        lam = (jnp.exp(jnp.sum(lam_q1[i].astype(jnp.float32) * lam_k1[i].astype(jnp.float32)))
               - jnp.exp(jnp.sum(lam_q2[i].astype(jnp.float32) * lam_k2[i].astype(jnp.float32)))
               + lam_init)
        o = diff_attention(q, k, v, lam)
        o = rms_norm(o, g_head[i].reshape(N_HEADS, D_V)) * (1.0 - lam_init)
        o = o.reshape(B, S, ATTN_WIDTH)

        pm = multiscale_pool(u).astype(x.dtype)
        pm = jnp.einsum('bsgc,gcd->bsgd', pm, w_pool[i]).reshape(B, S, POOL_WIDTH) * pool_scale[i]

        mix = jnp.concatenate([o, pm], axis=-1) @ w_out[i]
        x = x + rms_norm(mix, g_mix_post[i])

        h = rms_norm(x, g_ffn_pre[i])
        up = causal_dwconv(h @ w_up[i], conv_w[i], conv_b[i])
        gate, val = up[..., :D_FF], up[..., D_FF:]
        y = (jax.nn.gelu(gate, approximate=True) * val) @ w_down[i]
        x = x + rms_norm(y, g_ffn_post[i])
    return x
```

```python
import functools
import math

import jax
import jax.numpy as jnp
from jax import lax
from jax.experimental import pallas as pl
from jax.experimental.pallas import tpu as pltpu

N_HEADS = 8
D_V = 128
D_QK = 64
POOL_WINDOWS = (2, 4, 8, 16)
POOL_HALO = 16
CONV_WIDTH = 3
CONV_HALO = 8
EPS = 1e-6
LOG2E = math.log2(math.e)
NEG_BIG = -1e30

SEQ_TILE = 512
FFN_ROW_TILE = 1024
FFN_ROW_CHUNK = 256
FFN_COL_TILE = 512
DOWN_ROW_TILE = 256
VMEM_LIMIT_BYTES = 56 * 1024 * 1024

F32 = jnp.float32
BF16 = jnp.bfloat16


def _params(n_axes):
    return pltpu.CompilerParams(
        dimension_semantics=("arbitrary",) * n_axes,
        vmem_limit_bytes=VMEM_LIMIT_BYTES,
    )


def _resident(block_shape, index_map):
    return pl.BlockSpec(block_shape, index_map, pipeline_mode=pl.Buffered(1))


def _rms_scale(x):
    return lax.rsqrt(jnp.mean(x * x, axis=-1, keepdims=True) + EPS)


def _in_proj_kernel(x_ref, g_ref, w_ref, qT_ref, k_ref, vT_ref, u_ref, *, q_scale):
    x = x_ref[...]
    h = (x * _rms_scale(x) * g_ref[...]).astype(BF16)
    width = N_HEADS * D_V
    yq = jnp.dot(h, w_ref[:, 0:width], preferred_element_type=F32) * q_scale
    for hd in range(N_HEADS):
        qT_ref[0, hd, 0] = yq[:, hd * D_V:(hd + 1) * D_V].T.astype(BF16)
    yk = jnp.dot(h, w_ref[:, width:2 * width], preferred_element_type=F32)
    k_ref[...] = yk.astype(BF16)
    yv = jnp.dot(h, w_ref[:, 2 * width:3 * width], preferred_element_type=F32)
    for hd in range(N_HEADS):
        vT_ref[0, hd, 0] = yv[:, hd * D_V:(hd + 1) * D_V].T.astype(BF16)
    u_ref[...] = jnp.dot(h, w_ref[:, 3 * width:], preferred_element_type=F32)


def _in_proj(x2, g, w, batch, seq):
    m, d = x2.shape
    t = SEQ_TILE
    tiles = seq // t
    width = N_HEADS * D_V
    q_scale = (D_QK ** -0.5) * LOG2E
    t_shape = (batch, N_HEADS, tiles, D_V, t)
    t_spec = pl.BlockSpec((1, N_HEADS, 1, D_V, t), lambda i: (i // tiles, 0, i % tiles, 0, 0))
    return pl.pallas_call(
        functools.partial(_in_proj_kernel, q_scale=q_scale),
        grid=(m // t,),
        in_specs=[
            pl.BlockSpec((t, d), lambda i: (i, 0)),
            _resident((1, d), lambda i: (0, 0)),
            _resident((d, 4 * width), lambda i: (0, 0)),
        ],
        out_specs=[
            t_spec,
            pl.BlockSpec((t, width), lambda i: (i, 0)),
            t_spec,
            pl.BlockSpec((t, width), lambda i: (i, 0)),
        ],
        out_shape=[
            jax.ShapeDtypeStruct(t_shape, BF16),
            jax.ShapeDtypeStruct((m, width), BF16),
            jax.ShapeDtypeStruct(t_shape, BF16),
            jax.ShapeDtypeStruct((m, width), F32),
        ],
        compiler_params=_params(1),
        name="in_proj",
    )(x2, g, w)


def _attn_kernel(lq1_ref, lk1_ref, lq2_ref, lk2_ref, gh_ref, qT_ref, k_ref, vT_ref, o_ref,
                 brel_ref, bdiag_ref, acc1_ref, acc2_ref, *, lam_init):
    t = SEQ_TILE
    hd = pl.program_id(1)
    qi = pl.program_id(2)

    expo = jnp.full((1, t), -8.0 / N_HEADS, F32) * (hd + 1).astype(F32)
    slope = jnp.exp2(expo) * LOG2E

    @pl.when(qi == 0)
    def _():
        key = lax.broadcasted_iota(jnp.int32, (t, t), 0)
        qry = lax.broadcasted_iota(jnp.int32, (t, t), 1)
        rel = (key - qry).astype(F32) * slope
        brel_ref[...] = rel
        bdiag_ref[...] = jnp.where(key > qry, NEG_BIG, rel)

    qT = qT_ref[0, 0, 0]
    row = lax.broadcasted_iota(jnp.int32, (D_V, t), 0)
    zero = jnp.zeros_like(qT)
    qw1 = jnp.where(row < D_QK, qT, zero)
    qw2 = jnp.where(row < D_QK, zero, qT)

    acc1_ref[...] = jnp.zeros_like(acc1_ref)
    acc2_ref[...] = jnp.zeros_like(acc2_ref)

    def one_map(s, bias, off, m_old, l_old, vt, acc_ref):
        sb = s + bias
        m_new = jnp.maximum(m_old, jnp.max(sb, axis=0, keepdims=True) + off)
        e = jnp.exp2(sb - (m_new - off))
        alpha = jnp.exp2(m_old - m_new)
        l_new = alpha * l_old + jnp.sum(e, axis=0, keepdims=True)
        pv = jnp.dot(vt, e.astype(BF16), preferred_element_type=F32)
        acc_ref[...] = acc_ref[...] * alpha + pv
        return m_new, l_new

    def tile(j, bias_ref, off, carry):
        m1, l1, m2, l2 = carry
        start = pl.multiple_of(j * t, t)
        kt = k_ref[pl.ds(start, t), :]
        vt = vT_ref[0, 0, j]
        bias = bias_ref[...]
        s1 = jnp.dot(kt, qw1, preferred_element_type=F32)
        m1, l1 = one_map(s1, bias, off, m1, l1, vt, acc1_ref)
        s2 = jnp.dot(kt, qw2, preferred_element_type=F32)
        m2, l2 = one_map(s2, bias, off, m2, l2, vt, acc2_ref)
        return m1, l1, m2, l2

    def body(j, carry):
        off = slope * ((j - qi) * t).astype(F32)
        return tile(j, brel_ref, off, carry)

    neg = jnp.full((1, t), NEG_BIG, F32)
    zer = jnp.zeros((1, t), F32)
    carry = lax.fori_loop(0, qi, body, (neg, zer, neg, zer))
    m1, l1, m2, l2 = tile(qi, bdiag_ref, jnp.zeros((1, t), F32), carry)

    lam = (jnp.exp(jnp.sum(lq1_ref[...] * lk1_ref[...], axis=-1, keepdims=True))
           - jnp.exp(jnp.sum(lq2_ref[...] * lk2_ref[...], axis=-1, keepdims=True))
           + lam_init)
    o = acc1_ref[...] * (1.0 / l1) - lam * (acc2_ref[...] * (1.0 / l2))
    inv = lax.rsqrt(jnp.mean(o * o, axis=0, keepdims=True) + EPS)
    on = (o * inv) * gh_ref[0] * (1.0 - lam_init)
    o_ref[...] = on.T.astype(BF16)


def _attention(qT, k, vT, gh, lq1, lk1, lq2, lk2, batch, seq, lam_init):
    t = SEQ_TILE
    tiles = seq // t
    m, width = k.shape
    small = _resident((1, D_QK), lambda b, h, q: (0, 0))
    return pl.pallas_call(
        functools.partial(_attn_kernel, lam_init=lam_init),
        grid=(batch, N_HEADS, tiles),
        in_specs=[
            small, small, small, small,
            pl.BlockSpec((1, D_V, 1), lambda b, h, q: (h, 0, 0)),
            pl.BlockSpec((1, 1, 1, D_V, t), lambda b, h, q: (b, h, q, 0, 0)),
            pl.BlockSpec((seq, D_V), lambda b, h, q: (b, h)),
            pl.BlockSpec((1, 1, tiles, D_V, t), lambda b, h, q: (b, h, 0, 0, 0)),
        ],
        out_specs=pl.BlockSpec((t, D_V), lambda b, h, q: (b * tiles + q, h)),
        out_shape=jax.ShapeDtypeStruct((m, width), BF16),
        scratch_shapes=[
            pltpu.VMEM((t, t), F32),
            pltpu.VMEM((t, t), F32),
            pltpu.VMEM((D_V, t), F32),
            pltpu.VMEM((D_V, t), F32),
        ],
        compiler_params=_params(3),
        name="attn",
    )(lq1, lk1, lq2, lk2, gh, qT, k, vT)


def _mix_out_kernel(o_ref, u_ref, uh_ref, x_ref, wp_ref, ps_ref, wo_ref, gpost_ref, gpre_ref,
                    xo_ref, h_ref, *, tiles_per_seq):
    t = SEQ_TILE
    st = pl.program_id(0) % tiles_per_seq
    u = u_ref[...]
    halo = jnp.where(st == 0, 0.0, uh_ref[...])
    ext = jnp.concatenate([halo, u], axis=0)
    pos = (st * t + lax.broadcasted_iota(jnp.int32, (t, 1), 0) + 1).astype(F32)
    group = u.shape[1] // len(POOL_WINDOWS)
    attn_w = o_ref.shape[1]

    mix = jnp.dot(o_ref[...], wo_ref[0:attn_w, :], preferred_element_type=F32)
    for g, w in enumerate(POOL_WINDOWS):
        cols = slice(g * group, (g + 1) * group)
        s = ext[:, cols]
        span = 1
        while span < w:
            s = s + pltpu.roll(s, span, 0)
            span *= 2
        pm = s[POOL_HALO:, :] / jnp.minimum(pos, float(w)) - u[:, cols]
        pw = jnp.dot(pm.astype(BF16), wp_ref[g], preferred_element_type=F32) * ps_ref[:, cols]
        mix = mix + jnp.dot(pw.astype(BF16), wo_ref[attn_w + g * group:attn_w + (g + 1) * group, :],
                            preferred_element_type=F32)

    xn = x_ref[...] + mix * _rms_scale(mix) * gpost_ref[...]
    xo_ref[...] = xn
    h_ref[...] = (xn * _rms_scale(xn) * gpre_ref[...]).astype(BF16)


def _mix_out(o, u, x2, wp, ps, wo, gpost, gpre, seq):
    m, d = x2.shape
    t = SEQ_TILE
    tiles = seq // t
    pw = u.shape[1]
    halo_blocks = t // POOL_HALO
    const = lambda i: (0, 0)
    return pl.pallas_call(
        functools.partial(_mix_out_kernel, tiles_per_seq=tiles),
        grid=(m // t,),
        in_specs=[
            pl.BlockSpec((t, o.shape[1]), lambda i: (i, 0)),
            pl.BlockSpec((t, pw), lambda i: (i, 0)),
            pl.BlockSpec((POOL_HALO, pw), lambda i: (jnp.maximum(i * halo_blocks - 1, 0), 0)),
            pl.BlockSpec((t, d), lambda i: (i, 0)),
            _resident(wp.shape, lambda i: (0, 0, 0)),
            _resident((1, pw), const),
            _resident(wo.shape, const),
            _resident((1, d), const),
            _resident((1, d), const),
        ],
        out_specs=[
            pl.BlockSpec((t, d), lambda i: (i, 0)),
            pl.BlockSpec((t, d), lambda i: (i, 0)),
        ],
        out_shape=[
            jax.ShapeDtypeStruct((m, d), F32),
            jax.ShapeDtypeStruct((m, d), BF16),
        ],
        compiler_params=_params(1),
        name="mix_out",
    )(o, u, u, x2, wp, ps, wo, gpost, gpre)


def _ffn_up_kernel(h_ref, wg_ref, wv_ref, cwg_ref, cwv_ref, cbg_ref, cbv_ref, a_ref,
                   carry_g_ref, carry_v_ref, *, tiles_per_seq):
    st = pl.program_id(1) % tiles_per_seq

    @pl.when(st == 0)
    def _():
        carry_g_ref[...] = jnp.zeros_like(carry_g_ref)
        carry_v_ref[...] = jnp.zeros_like(carry_v_ref)

    def conv(up, prev, cw_ref, cb_ref):
        ext = jnp.concatenate([prev, up], axis=0)
        x1 = pltpu.roll(ext, 1, 0)[CONV_HALO:, :]
        x2 = pltpu.roll(ext, 2, 0)[CONV_HALO:, :]
        return cb_ref[...] + x2 * cw_ref[0:1, :] + x1 * cw_ref[1:2, :] + up * cw_ref[2:3, :]

    prev_g = carry_g_ref[...]
    prev_v = carry_v_ref[...]
    rows = h_ref.shape[0]
    for r in range(0, rows, FFN_ROW_CHUNK):
        h = h_ref[r:r + FFN_ROW_CHUNK, :]
        up_g = jnp.dot(h, wg_ref[...], preferred_element_type=F32)
        up_v = jnp.dot(h, wv_ref[...], preferred_element_type=F32)
        gate = conv(up_g, prev_g, cwg_ref, cbg_ref)
        val = conv(up_v, prev_v, cwv_ref, cbv_ref)
        a_ref[r:r + FFN_ROW_CHUNK, :] = (jax.nn.gelu(gate, approximate=True) * val).astype(BF16)
        prev_g = up_g[FFN_ROW_CHUNK - CONV_HALO:, :]
        prev_v = up_v[FFN_ROW_CHUNK - CONV_HALO:, :]
    carry_g_ref[...] = prev_g
    carry_v_ref[...] = prev_v


def _ffn_up(h, w_up, conv_w, conv_b, seq):
    m, d = h.shape
    d_ff = w_up.shape[1] // 2
    tm, tn = FFN_ROW_TILE, FFN_COL_TILE
    nb = d_ff // tn
    tiles = seq // tm
    return pl.pallas_call(
        functools.partial(_ffn_up_kernel, tiles_per_seq=tiles),
        grid=(nb, m // tm),
        in_specs=[
            pl.BlockSpec((tm, d), lambda n, i: (i, 0)),
            pl.BlockSpec((d, tn), lambda n, i: (0, n)),
            pl.BlockSpec((d, tn), lambda n, i: (0, n + nb)),
            pl.BlockSpec((CONV_WIDTH, tn), lambda n, i: (0, n)),
            pl.BlockSpec((CONV_WIDTH, tn), lambda n, i: (0, n + nb)),
            pl.BlockSpec((1, tn), lambda n, i: (0, n)),
            pl.BlockSpec((1, tn), lambda n, i: (0, n + nb)),
        ],
        out_specs=pl.BlockSpec((tm, tn), lambda n, i: (i, n)),
        out_shape=jax.ShapeDtypeStruct((m, d_ff), BF16),
        scratch_shapes=[
            pltpu.VMEM((CONV_HALO, tn), F32),
            pltpu.VMEM((CONV_HALO, tn), F32),
        ],
        compiler_params=_params(2),
        name="ffn_up",
    )(h, w_up, w_up, conv_w, conv_w, conv_b, conv_b)


def _ffn_down_kernel(a_ref, w_ref, x_ref, g_ref, xo_ref):
    y = jnp.dot(a_ref[...], w_ref[...], preferred_element_type=F32)
    xo_ref[...] = x_ref[...] + y * _rms_scale(y) * g_ref[...]


def _ffn_down(a, w, x2, g):
    m, d = x2.shape
    d_ff = a.shape[1]
    tm = DOWN_ROW_TILE
    return pl.pallas_call(
        _ffn_down_kernel,
        grid=(m // tm,),
        in_specs=[
            pl.BlockSpec((tm, d_ff), lambda i: (i, 0)),
            _resident((d_ff, d), lambda i: (0, 0)),
            pl.BlockSpec((tm, d), lambda i: (i, 0)),
            _resident((1, d), lambda i: (0, 0)),
        ],
        out_specs=pl.BlockSpec((tm, d), lambda i: (i, 0)),
        out_shape=jax.ShapeDtypeStruct((m, d), F32),
        compiler_params=_params(1),
        name="ffn_down",
    )(a, w, x2, g)


def kernel(x, g_mix_pre, w_in, lam_q1, lam_k1, lam_q2, lam_k2, g_head, w_pool, pool_scale,
           w_out, g_mix_post, g_ffn_pre, w_up, conv_w, conv_b, w_down, g_ffn_post):
    batch, seq, d = x.shape
    depth = w_in.shape[0]
    assert seq % FFN_ROW_TILE == 0 and seq % SEQ_TILE == 0
    assert g_head.shape[1] == N_HEADS * D_V
    assert (w_up.shape[2] // 2) % FFN_COL_TILE == 0

    w_in_b = w_in.astype(BF16)
    w_pool_b = w_pool.astype(BF16)
    w_out_b = w_out.astype(BF16)
    w_up_b = w_up.astype(BF16)
    w_down_b = w_down.astype(BF16)

    x2 = x.reshape(batch * seq, d)
    for i in range(depth):
        lam_init = 0.8 - 0.6 * math.exp(-0.3 * i)
        qT, k, vT, u = _in_proj(x2, g_mix_pre[i][None], w_in_b[i], batch, seq)
        o = _attention(qT, k, vT, g_head[i].reshape(N_HEADS, D_V, 1),
                       lam_q1[i][None], lam_k1[i][None], lam_q2[i][None], lam_k2[i][None],
                       batch, seq, lam_init)
        x2, h = _mix_out(o, u, x2, w_pool_b[i], pool_scale[i][None], w_out_b[i],
                         g_mix_post[i][None], g_ffn_pre[i][None], seq)
        a = _ffn_up(h, w_up_b[i], conv_w[i], conv_b[i][None], seq)
        x2 = _ffn_down(a, w_down_b[i], x2, g_ffn_post[i][None])
    return x2.reshape(batch, seq, d)
```

```python
import functools
import math

import jax
import jax.numpy as jnp
from jax import lax
from jax.experimental import pallas as pl
from jax.experimental.pallas import tpu as pltpu

N_HEADS = 8
D_V = 128
D_QK = 64
POOL_WINDOWS = (2, 4, 8, 16)
POOL_HALO = 16
CONV_WIDTH = 3
CONV_HALO = 8
EPS = 1e-6
LOG2E = math.log2(math.e)
NEG_BIG = -1e30

SEQ_TILE = 512
ATTN_STRIP = 256
FFN_ROW_TILE = 1024
FFN_ROW_CHUNK = 256
FFN_COL_TILE = 512
DOWN_ROW_TILE = 256
VMEM_LIMIT_BYTES = 56 * 1024 * 1024

F32 = jnp.float32
BF16 = jnp.bfloat16


def _params(n_axes):
    return pltpu.CompilerParams(
        dimension_semantics=("arbitrary",) * n_axes,
        vmem_limit_bytes=VMEM_LIMIT_BYTES,
    )


def _resident(block_shape, index_map):
    return pl.BlockSpec(block_shape, index_map, pipeline_mode=pl.Buffered(1))


def _rms_scale(x):
    return lax.rsqrt(jnp.mean(x * x, axis=-1, keepdims=True) + EPS)


def _in_proj_kernel(x_ref, g_ref, w_ref, qT_ref, k_ref, vT_ref, u_ref, *, q_scale):
    x = x_ref[...]
    h = (x * _rms_scale(x) * g_ref[...]).astype(BF16)
    width = N_HEADS * D_V
    yq = jnp.dot(h, w_ref[:, 0:width], preferred_element_type=F32) * q_scale
    for hd in range(N_HEADS):
        qT_ref[0, hd, 0] = yq[:, hd * D_V:(hd + 1) * D_V].T.astype(BF16)
    yk = jnp.dot(h, w_ref[:, width:2 * width], preferred_element_type=F32)
    k_ref[...] = yk.astype(BF16)
    yv = jnp.dot(h, w_ref[:, 2 * width:3 * width], preferred_element_type=F32)
    for hd in range(N_HEADS):
        vT_ref[0, hd, 0] = yv[:, hd * D_V:(hd + 1) * D_V].T.astype(BF16)
    u_ref[...] = jnp.dot(h, w_ref[:, 3 * width:], preferred_element_type=F32)


def _in_proj(x2, g, w, batch, seq):
    m, d = x2.shape
    t = SEQ_TILE
    tiles = seq // t
    width = N_HEADS * D_V
    q_scale = (D_QK ** -0.5) * LOG2E
    t_shape = (batch, N_HEADS, tiles, D_V, t)
    t_spec = pl.BlockSpec((1, N_HEADS, 1, D_V, t), lambda i: (i // tiles, 0, i % tiles, 0, 0))
    return pl.pallas_call(
        functools.partial(_in_proj_kernel, q_scale=q_scale),
        grid=(m // t,),
        in_specs=[
            pl.BlockSpec((t, d), lambda i: (i, 0)),
            _resident((1, d), lambda i: (0, 0)),
            _resident((d, 4 * width), lambda i: (0, 0)),
        ],
        out_specs=[
            t_spec,
            pl.BlockSpec((t, width), lambda i: (i, 0)),
            t_spec,
            pl.BlockSpec((t, width), lambda i: (i, 0)),
        ],
        out_shape=[
            jax.ShapeDtypeStruct(t_shape, BF16),
            jax.ShapeDtypeStruct((m, width), BF16),
            jax.ShapeDtypeStruct(t_shape, BF16),
            jax.ShapeDtypeStruct((m, width), F32),
        ],
        compiler_params=_params(1),
        name="in_proj",
    )(x2, g, w)


def _attn_kernel(lq1_ref, lk1_ref, lq2_ref, lk2_ref, gh_ref, qT_ref, k_ref, vT_ref, o_ref,
                 bias_ref, s_ref, acc_ref, *, lam_init):
    t = SEQ_TILE
    w = ATTN_STRIP
    strips = [slice(c, c + w) for c in range(0, t, w)]
    hd = pl.program_id(1)
    qi = pl.program_id(2)

    expo = jnp.full((1, t), -8.0 / N_HEADS, F32) * (hd + 1).astype(F32)
    slope = jnp.exp2(expo) * LOG2E

    @pl.when(qi == 0)
    def _():
        key = lax.broadcasted_iota(jnp.int32, (t, t), 0)
        qry = lax.broadcasted_iota(jnp.int32, (t, t), 1)
        rel = (key - qry).astype(F32) * slope
        bias_ref[0] = rel
        bias_ref[1] = jnp.where(key > qry, NEG_BIG, rel)

    qT = qT_ref[0, 0, 0]
    row = lax.broadcasted_iota(jnp.int32, (D_V, t), 0)
    zero = jnp.zeros_like(qT)
    qw = (jnp.where(row < D_QK, qT, zero), jnp.where(row < D_QK, zero, qT))

    acc_ref[...] = jnp.zeros_like(acc_ref)

    def logits(j):
        start = pl.multiple_of(j * t, t)
        kt = k_ref[pl.ds(start, t), :]
        diag = (j == qi).astype(jnp.int32)
        cms = []
        for mp in range(2):
            parts = []
            for cols in strips:
                sb = (jnp.dot(kt, qw[mp][:, cols], preferred_element_type=F32)
                      + bias_ref[diag, :, cols])
                s_ref[mp, :, cols] = sb
                parts.append(jnp.max(sb, axis=0, keepdims=True))
            cms.append(jnp.concatenate(parts, axis=1))
        return cms

    def values(j, off, cms, stats):
        vt = vT_ref[0, 0, j]
        out = []
        for mp in range(2):
            m_old, l_old = stats[mp]
            m_new = jnp.maximum(m_old, cms[mp] + off)
            mm = m_new - off
            alpha = jnp.exp2(m_old - m_new)
            sums = []
            for cols in strips:
                e = jnp.exp2(s_ref[mp, :, cols] - mm[:, cols])
                sums.append(jnp.sum(e, axis=0, keepdims=True))
                pv = jnp.dot(vt, e.astype(BF16), preferred_element_type=F32)
                acc_ref[mp, :, cols] = acc_ref[mp, :, cols] * alpha[:, cols] + pv
            out.append((m_new, alpha * l_old + jnp.concatenate(sums, axis=1)))
        return out

    def body(j, carry):
        cm1, cm2, m1, l1, m2, l2 = carry
        off = slope * ((j - qi) * t).astype(F32)
        (m1, l1), (m2, l2) = values(j, off, (cm1, cm2), ((m1, l1), (m2, l2)))
        cm1, cm2 = logits(j + 1)
        return cm1, cm2, m1, l1, m2, l2

    neg = jnp.full((1, t), NEG_BIG, F32)
    zer = jnp.zeros((1, t), F32)
    cm1, cm2 = logits(0)
    cm1, cm2, m1, l1, m2, l2 = lax.fori_loop(0, qi, body, (cm1, cm2, neg, zer, neg, zer))
    (m1, l1), (m2, l2) = values(qi, zer, (cm1, cm2), ((m1, l1), (m2, l2)))

    lam = (jnp.exp(jnp.sum(lq1_ref[...] * lk1_ref[...], axis=-1, keepdims=True))
           - jnp.exp(jnp.sum(lq2_ref[...] * lk2_ref[...], axis=-1, keepdims=True))
           + lam_init)
    o = acc_ref[0] * (1.0 / l1) - lam * (acc_ref[1] * (1.0 / l2))
    inv = lax.rsqrt(jnp.mean(o * o, axis=0, keepdims=True) + EPS)
    on = (o * inv) * gh_ref[0] * (1.0 - lam_init)
    o_ref[...] = on.T.astype(BF16)


def _attention(qT, k, vT, gh, lq1, lk1, lq2, lk2, batch, seq, lam_init):
    t = SEQ_TILE
    tiles = seq // t
    m, width = k.shape
    small = _resident((1, D_QK), lambda b, h, q: (0, 0))
    return pl.pallas_call(
        functools.partial(_attn_kernel, lam_init=lam_init),
        grid=(batch, N_HEADS, tiles),
        in_specs=[
            small, small, small, small,
            pl.BlockSpec((1, D_V, 1), lambda b, h, q: (h, 0, 0)),
            pl.BlockSpec((1, 1, 1, D_V, t), lambda b, h, q: (b, h, q, 0, 0)),
            pl.BlockSpec((seq, D_V), lambda b, h, q: (b, h)),
            pl.BlockSpec((1, 1, tiles, D_V, t), lambda b, h, q: (b, h, 0, 0, 0)),
        ],
        out_specs=pl.BlockSpec((t, D_V), lambda b, h, q: (b * tiles + q, h)),
        out_shape=jax.ShapeDtypeStruct((m, width), BF16),
        scratch_shapes=[
            pltpu.VMEM((2, t, t), F32),
            pltpu.VMEM((2, t, t), F32),
            pltpu.VMEM((2, D_V, t), F32),
        ],
        compiler_params=_params(3),
        name="attn",
    )(lq1, lk1, lq2, lk2, gh, qT, k, vT)


def _mix_out_kernel(o_ref, u_ref, uh_ref, x_ref, wp_ref, ps_ref, wo_ref, gpost_ref, gpre_ref,
                    xo_ref, h_ref, *, tiles_per_seq):
    t = SEQ_TILE
    st = pl.program_id(0) % tiles_per_seq
    u = u_ref[...]
    halo = jnp.where(st == 0, 0.0, uh_ref[...])
    ext = jnp.concatenate([halo, u], axis=0)
    pos = (st * t + lax.broadcasted_iota(jnp.int32, (t, 1), 0) + 1).astype(F32)
    group = u.shape[1] // len(POOL_WINDOWS)
    attn_w = o_ref.shape[1]

    mix = jnp.dot(o_ref[...], wo_ref[0:attn_w, :], preferred_element_type=F32)
    for g, w in enumerate(POOL_WINDOWS):
        cols = slice(g * group, (g + 1) * group)
        s = ext[:, cols]
        span = 1
        while span < w:
            s = s + pltpu.roll(s, span, 0)
            span *= 2
        pm = s[POOL_HALO:, :] / jnp.minimum(pos, float(w)) - u[:, cols]
        pw = jnp.dot(pm.astype(BF16), wp_ref[g], preferred_element_type=F32) * ps_ref[:, cols]
        mix = mix + jnp.dot(pw.astype(BF16), wo_ref[attn_w + g * group:attn_w + (g + 1) * group, :],
                            preferred_element_type=F32)

    xn = x_ref[...] + mix * _rms_scale(mix) * gpost_ref[...]
    xo_ref[...] = xn
    h_ref[...] = (xn * _rms_scale(xn) * gpre_ref[...]).astype(BF16)


def _mix_out(o, u, x2, wp, ps, wo, gpost, gpre, seq):
    m, d = x2.shape
    t = SEQ_TILE
    tiles = seq // t
    pw = u.shape[1]
    halo_blocks = t // POOL_HALO
    const = lambda i: (0, 0)
    return pl.pallas_call(
        functools.partial(_mix_out_kernel, tiles_per_seq=tiles),
        grid=(m // t,),
        in_specs=[
            pl.BlockSpec((t, o.shape[1]), lambda i: (i, 0)),
            pl.BlockSpec((t, pw), lambda i: (i, 0)),
            pl.BlockSpec((POOL_HALO, pw), lambda i: (jnp.maximum(i * halo_blocks - 1, 0), 0)),
            pl.BlockSpec((t, d), lambda i: (i, 0)),
            _resident(wp.shape, lambda i: (0, 0, 0)),
            _resident((1, pw), const),
            _resident(wo.shape, const),
            _resident((1, d), const),
            _resident((1, d), const),
        ],
        out_specs=[
            pl.BlockSpec((t, d), lambda i: (i, 0)),
            pl.BlockSpec((t, d), lambda i: (i, 0)),
        ],
        out_shape=[
            jax.ShapeDtypeStruct((m, d), F32),
            jax.ShapeDtypeStruct((m, d), BF16),
        ],
        compiler_params=_params(1),
        name="mix_out",
    )(o, u, u, x2, wp, ps, wo, gpost, gpre)


def _ffn_up_kernel(h_ref, wg_ref, wv_ref, cwg_ref, cwv_ref, cbg_ref, cbv_ref, a_ref,
                   carry_g_ref, carry_v_ref, *, tiles_per_seq):
    st = pl.program_id(1) % tiles_per_seq

    @pl.when(st == 0)
    def _():
        carry_g_ref[...] = jnp.zeros_like(carry_g_ref)
        carry_v_ref[...] = jnp.zeros_like(carry_v_ref)

    def conv(up, prev, cw_ref, cb_ref):
        ext = jnp.concatenate([prev, up], axis=0)
        x1 = pltpu.roll(ext, 1, 0)[CONV_HALO:, :]
        x2 = pltpu.roll(ext, 2, 0)[CONV_HALO:, :]
        return cb_ref[...] + x2 * cw_ref[0:1, :] + x1 * cw_ref[1:2, :] + up * cw_ref[2:3, :]

    prev_g = carry_g_ref[...]
    prev_v = carry_v_ref[...]
    rows = h_ref.shape[0]
    for r in range(0, rows, FFN_ROW_CHUNK):
        h = h_ref[r:r + FFN_ROW_CHUNK, :]
        up_g = jnp.dot(h, wg_ref[...], preferred_element_type=F32)
        up_v = jnp.dot(h, wv_ref[...], preferred_element_type=F32)
        gate = conv(up_g, prev_g, cwg_ref, cbg_ref)
        val = conv(up_v, prev_v, cwv_ref, cbv_ref)
        a_ref[r:r + FFN_ROW_CHUNK, :] = (jax.nn.gelu(gate, approximate=True) * val).astype(BF16)
        prev_g = up_g[FFN_ROW_CHUNK - CONV_HALO:, :]
        prev_v = up_v[FFN_ROW_CHUNK - CONV_HALO:, :]
    carry_g_ref[...] = prev_g
    carry_v_ref[...] = prev_v


def _ffn_up(h, w_up, conv_w, conv_b, seq):
    m, d = h.shape
    d_ff = w_up.shape[1] // 2
    tm, tn = FFN_ROW_TILE, FFN_COL_TILE
    nb = d_ff // tn
    tiles = seq // tm
    return pl.pallas_call(
        functools.partial(_ffn_up_kernel, tiles_per_seq=tiles),
        grid=(nb, m // tm),
        in_specs=[
            pl.BlockSpec((tm, d), lambda n, i: (i, 0)),
            pl.BlockSpec((d, tn), lambda n, i: (0, n)),
            pl.BlockSpec((d, tn), lambda n, i: (0, n + nb)),
            pl.BlockSpec((CONV_WIDTH, tn), lambda n, i: (0, n)),
            pl.BlockSpec((CONV_WIDTH, tn), lambda n, i: (0, n + nb)),
            pl.BlockSpec((1, tn), lambda n, i: (0, n)),
            pl.BlockSpec((1, tn), lambda n, i: (0, n + nb)),
        ],
        out_specs=pl.BlockSpec((tm, tn), lambda n, i: (i, n)),
        out_shape=jax.ShapeDtypeStruct((m, d_ff), BF16),
        scratch_shapes=[
            pltpu.VMEM((CONV_HALO, tn), F32),
            pltpu.VMEM((CONV_HALO, tn), F32),
        ],
        compiler_params=_params(2),
        name="ffn_up",
    )(h, w_up, w_up, conv_w, conv_w, conv_b, conv_b)


def _ffn_down_kernel(a_ref, w_ref, x_ref, g_ref, xo_ref):
    y = jnp.dot(a_ref[...], w_ref[...], preferred_element_type=F32)
    xo_ref[...] = x_ref[...] + y * _rms_scale(y) * g_ref[...]


def _ffn_down(a, w, x2, g):
    m, d = x2.shape
    d_ff = a.shape[1]
    tm = DOWN_ROW_TILE
    return pl.pallas_call(
        _ffn_down_kernel,
        grid=(m // tm,),
        in_specs=[
            pl.BlockSpec((tm, d_ff), lambda i: (i, 0)),
            _resident((d_ff, d), lambda i: (0, 0)),
            pl.BlockSpec((tm, d), lambda i: (i, 0)),
            _resident((1, d), lambda i: (0, 0)),
        ],
        out_specs=pl.BlockSpec((tm, d), lambda i: (i, 0)),
        out_shape=jax.ShapeDtypeStruct((m, d), F32),
        compiler_params=_params(1),
        name="ffn_down",
    )(a, w, x2, g)


def kernel(x, g_mix_pre, w_in, lam_q1, lam_k1, lam_q2, lam_k2, g_head, w_pool, pool_scale,
           w_out, g_mix_post, g_ffn_pre, w_up, conv_w, conv_b, w_down, g_ffn_post):
    batch, seq, d = x.shape
    depth = w_in.shape[0]
    assert seq % FFN_ROW_TILE == 0 and seq % SEQ_TILE == 0
    assert g_head.shape[1] == N_HEADS * D_V
    assert (w_up.shape[2] // 2) % FFN_COL_TILE == 0

    w_in_b = w_in.astype(BF16)
    w_pool_b = w_pool.astype(BF16)
    w_out_b = w_out.astype(BF16)
    w_up_b = w_up.astype(BF16)
    w_down_b = w_down.astype(BF16)

    x2 = x.reshape(batch * seq, d)
    for i in range(depth):
        lam_init = 0.8 - 0.6 * math.exp(-0.3 * i)
        qT, k, vT, u = _in_proj(x2, g_mix_pre[i][None], w_in_b[i], batch, seq)
        o = _attention(qT, k, vT, g_head[i].reshape(N_HEADS, D_V, 1),
                       lam_q1[i][None], lam_k1[i][None], lam_q2[i][None], lam_k2[i][None],
                       batch, seq, lam_init)
        x2, h = _mix_out(o, u, x2, w_pool_b[i], pool_scale[i][None], w_out_b[i],
                         g_mix_post[i][None], g_ffn_pre[i][None], seq)
        a = _ffn_up(h, w_up_b[i], conv_w[i], conv_b[i][None], seq)
        x2 = _ffn_down(a, w_down_b[i], x2, g_ffn_post[i][None])
    return x2.reshape(batch, seq, d)
```

```python
import functools
import math

import jax
import jax.numpy as jnp
from jax import lax
from jax.experimental import pallas as pl
from jax.experimental.pallas import tpu as pltpu

N_HEADS = 8
D_V = 128
D_QK = 64
POOL_WINDOWS = (2, 4, 8, 16)
POOL_HALO = 16
CONV_WIDTH = 3
CONV_HALO = 8
EPS = 1e-6
LOG2E = math.log2(math.e)
NEG_BIG = -1e30

SEQ_TILE = 512
ATTN_STRIP = 256
ATTN_SUM_ROWS = 16
FFN_ROW_TILE = 1024
FFN_ROW_CHUNK = 256
FFN_COL_TILE = 512
DOWN_ROW_TILE = 256
VMEM_LIMIT_BYTES = 56 * 1024 * 1024

F32 = jnp.float32
BF16 = jnp.bfloat16


def _params(n_axes):
    return pltpu.CompilerParams(
        dimension_semantics=("arbitrary",) * n_axes,
        vmem_limit_bytes=VMEM_LIMIT_BYTES,
    )


def _resident(block_shape, index_map):
    return pl.BlockSpec(block_shape, index_map, pipeline_mode=pl.Buffered(1))


def _rms_scale(x):
    return lax.rsqrt(jnp.mean(x * x, axis=-1, keepdims=True) + EPS)


def _in_proj_kernel(x_ref, g_ref, w_ref, qT_ref, k_ref, vT_ref, u_ref, *, q_scale):
    x = x_ref[...]
    h = (x * _rms_scale(x) * g_ref[...]).astype(BF16)
    width = N_HEADS * D_V
    yq = jnp.dot(h, w_ref[:, 0:width], preferred_element_type=F32) * q_scale
    for hd in range(N_HEADS):
        qT_ref[0, hd, 0] = yq[:, hd * D_V:(hd + 1) * D_V].T.astype(BF16)
    yk = jnp.dot(h, w_ref[:, width:2 * width], preferred_element_type=F32)
    k_ref[...] = yk.astype(BF16)
    yv = jnp.dot(h, w_ref[:, 2 * width:3 * width], preferred_element_type=F32)
    for hd in range(N_HEADS):
        vT_ref[0, hd, 0] = yv[:, hd * D_V:(hd + 1) * D_V].T.astype(BF16)
    u_ref[...] = jnp.dot(h, w_ref[:, 3 * width:], preferred_element_type=F32)


def _in_proj(x2, g, w, batch, seq):
    m, d = x2.shape
    t = SEQ_TILE
    tiles = seq // t
    width = N_HEADS * D_V
    q_scale = (D_QK ** -0.5) * LOG2E
    t_shape = (batch, N_HEADS, tiles, D_V, t)
    t_spec = pl.BlockSpec((1, N_HEADS, 1, D_V, t), lambda i: (i // tiles, 0, i % tiles, 0, 0))
    return pl.pallas_call(
        functools.partial(_in_proj_kernel, q_scale=q_scale),
        grid=(m // t,),
        in_specs=[
            pl.BlockSpec((t, d), lambda i: (i, 0)),
            _resident((1, d), lambda i: (0, 0)),
            _resident((d, 4 * width), lambda i: (0, 0)),
        ],
        out_specs=[
            t_spec,
            pl.BlockSpec((t, width), lambda i: (i, 0)),
            t_spec,
            pl.BlockSpec((t, width), lambda i: (i, 0)),
        ],
        out_shape=[
            jax.ShapeDtypeStruct(t_shape, BF16),
            jax.ShapeDtypeStruct((m, width), BF16),
            jax.ShapeDtypeStruct(t_shape, BF16),
            jax.ShapeDtypeStruct((m, width), F32),
        ],
        compiler_params=_params(1),
        name="in_proj",
    )(x2, g, w)


def _split3(v):
    a = v.astype(BF16).astype(F32)
    b = (v - a).astype(BF16).astype(F32)
    c = (v - a - b).astype(BF16).astype(F32)
    return a, b, c


def _attn_kernel(lq1_ref, lk1_ref, lq2_ref, lk2_ref, gh_ref, qT_ref, k_ref, vT_ref, o_ref,
                 pf_ref, qf_ref, mask_ref, s_ref, acc_ref, stat_ref, *, lam_init):
    t = SEQ_TILE
    w = ATTN_STRIP
    strips = [slice(c, c + w) for c in range(0, t, w)]
    hd = pl.program_id(1)
    qi = pl.program_id(2)

    expo = jnp.full((1, t), -8.0 / N_HEADS, F32) * (hd + 1).astype(F32)
    slope = jnp.exp2(expo) * LOG2E

    @pl.when(qi == 0)
    def _():
        c = lax.broadcasted_iota(jnp.int32, (t, D_V), 0)
        lane = lax.broadcasted_iota(jnp.int32, (t, D_V), 1)
        c_hi = ((c >> 8) << 8).astype(F32)
        c_lo = (c & 255).astype(F32)
        pf = jnp.where(lane < 3, c_hi, jnp.where(lane < 6, c_lo, jnp.where(lane < 9, 1.0, 0.0)))
        pf_ref[...] = pf.astype(BF16)
        r = lax.broadcasted_iota(jnp.int32, (1, t), 1).astype(F32)
        sp = _split3(slope)
        gp = _split3(-slope * r)
        row = lax.broadcasted_iota(jnp.int32, (D_V, t), 0)
        qf = jnp.zeros((D_V, t), F32)
        for i in range(3):
            qf = jnp.where((row == i) | (row == i + 3), sp[i], qf)
            qf = jnp.where(row == i + 6, gp[i], qf)
        qf_ref[...] = qf.astype(BF16)
        key = lax.broadcasted_iota(jnp.int32, (t, t), 0)
        qry = lax.broadcasted_iota(jnp.int32, (t, t), 1)
        mask_ref[...] = jnp.where(key > qry, NEG_BIG, 0.0)

    qT = qT_ref[0, 0, 0]
    row = lax.broadcasted_iota(jnp.int32, (D_V, t), 0)
    zero = jnp.zeros_like(qT)
    qf = qf_ref[...]
    qw = (jnp.concatenate([jnp.where(row < D_QK, qT, zero), qf], axis=0),
          jnp.concatenate([jnp.where(row < D_QK, zero, qT), qf], axis=0))

    acc_ref[...] = jnp.zeros_like(acc_ref)
    stat_ref[0:2] = jnp.full((2, 1, t), NEG_BIG, F32)

    def logits(mp, j, diag):
        start = pl.multiple_of(j * t, t)
        kt = jnp.concatenate([k_ref[pl.ds(start, t), :], pf_ref[...]], axis=1)
        parts = []
        for cols in strips:
            sb = jnp.dot(kt, qw[mp][:, cols], preferred_element_type=F32)
            if diag:
                sb = sb + mask_ref[:, cols]
            s_ref[mp, :, cols] = sb
            parts.append(jnp.max(sb, axis=0, keepdims=True))
        stat_ref[2 + mp] = jnp.concatenate(parts, axis=1)

    ones = jnp.ones((ATTN_SUM_ROWS, t), BF16)

    def values(mp, j):
        off = slope * ((j - qi) * t).astype(F32)
        vt = jnp.concatenate([vT_ref[0, 0, j], ones], axis=0)
        m_old = stat_ref[mp]
        m_new = jnp.maximum(m_old, stat_ref[2 + mp] + off)
        stat_ref[mp] = m_new
        mm = m_new - off
        alpha = jnp.exp2(m_old - m_new)
        for cols in strips:
            e = jnp.exp2(s_ref[mp, :, cols] - mm[:, cols]).astype(BF16)
            pv = jnp.dot(vt, e, preferred_element_type=F32)
            acc_ref[mp, :, cols] = acc_ref[mp, :, cols] * alpha[:, cols] + pv

    @pl.when(qi == 0)
    def _():
        logits(0, 0, True)

    @pl.when(qi > 0)
    def _():
        logits(0, 0, False)

    def step(j, last):
        logits(1, j, False)
        values(0, j)
        logits(0, j + 1, last)
        values(1, j)

    n_plain = jnp.maximum(qi - 1, 0)

    def body(p, carry):
        step(2 * p, False)
        step(2 * p + 1, False)
        return carry

    lax.fori_loop(0, n_plain // 2, body, 0)

    @pl.when(n_plain % 2 == 1)
    def _():
        step(n_plain - 1, False)

    @pl.when(qi > 0)
    def _():
        step(qi - 1, True)

    logits(1, qi, True)
    values(0, qi)
    values(1, qi)

    lam = (jnp.exp(jnp.sum(lq1_ref[...] * lk1_ref[...], axis=-1, keepdims=True))
           - jnp.exp(jnp.sum(lq2_ref[...] * lk2_ref[...], axis=-1, keepdims=True))
           + lam_init)
    l1 = acc_ref[0, D_V:D_V + 1, :]
    l2 = acc_ref[1, D_V:D_V + 1, :]
    o = acc_ref[0, 0:D_V, :] * (1.0 / l1) - lam * (acc_ref[1, 0:D_V, :] * (1.0 / l2))
    inv = lax.rsqrt(jnp.mean(o * o, axis=0, keepdims=True) + EPS)
    on = (o * inv) * gh_ref[0] * (1.0 - lam_init)
    o_ref[...] = on.T.astype(BF16)


def _attention(qT, k, vT, gh, lq1, lk1, lq2, lk2, batch, seq, lam_init):
    t = SEQ_TILE
    tiles = seq // t
    m, width = k.shape
    small = _resident((1, D_QK), lambda b, h, q: (0, 0))
    return pl.pallas_call(
        functools.partial(_attn_kernel, lam_init=lam_init),
        grid=(batch, N_HEADS, tiles),
        in_specs=[
            small, small, small, small,
            pl.BlockSpec((1, D_V, 1), lambda b, h, q: (h, 0, 0)),
            pl.BlockSpec((1, 1, 1, D_V, t), lambda b, h, q: (b, h, q, 0, 0)),
            pl.BlockSpec((seq, D_V), lambda b, h, q: (b, h)),
            pl.BlockSpec((1, 1, tiles, D_V, t), lambda b, h, q: (b, h, 0, 0, 0)),
        ],
        out_specs=pl.BlockSpec((t, D_V), lambda b, h, q: (b * tiles + q, h)),
        out_shape=jax.ShapeDtypeStruct((m, width), BF16),
        scratch_shapes=[
            pltpu.VMEM((t, D_V), BF16),
            pltpu.VMEM((D_V, t), BF16),
            pltpu.VMEM((t, t), F32),
            pltpu.VMEM((2, t, t), F32),
            pltpu.VMEM((2, D_V + ATTN_SUM_ROWS, t), F32),
            pltpu.VMEM((4, 1, t), F32),
        ],
        compiler_params=_params(3),
        name="attn",
    )(lq1, lk1, lq2, lk2, gh, qT, k, vT)


def _mix_out_kernel(o_ref, u_ref, uh_ref, x_ref, wp_ref, ps_ref, wo_ref, gpost_ref, gpre_ref,
                    xo_ref, h_ref, *, tiles_per_seq):
    t = SEQ_TILE
    st = pl.program_id(0) % tiles_per_seq
    u = u_ref[...]
    halo = jnp.where(st == 0, 0.0, uh_ref[...])
    ext = jnp.concatenate([halo, u], axis=0)
    pos = (st * t + lax.broadcasted_iota(jnp.int32, (t, 1), 0) + 1).astype(F32)
    group = u.shape[1] // len(POOL_WINDOWS)
    attn_w = o_ref.shape[1]

    mix = jnp.dot(o_ref[...], wo_ref[0:attn_w, :], preferred_element_type=F32)
    for g, w in enumerate(POOL_WINDOWS):
        cols = slice(g * group, (g + 1) * group)
        s = ext[:, cols]
        span = 1
        while span < w:
            s = s + pltpu.roll(s, span, 0)
            span *= 2
        pm = s[POOL_HALO:, :] / jnp.minimum(pos, float(w)) - u[:, cols]
        pw = jnp.dot(pm.astype(BF16), wp_ref[g], preferred_element_type=F32) * ps_ref[:, cols]
        mix = mix + jnp.dot(pw.astype(BF16), wo_ref[attn_w + g * group:attn_w + (g + 1) * group, :],
                            preferred_element_type=F32)

    xn = x_ref[...] + mix * _rms_scale(mix) * gpost_ref[...]
    xo_ref[...] = xn
    h_ref[...] = (xn * _rms_scale(xn) * gpre_ref[...]).astype(BF16)


def _mix_out(o, u, x2, wp, ps, wo, gpost, gpre, seq):
    m, d = x2.shape
    t = SEQ_TILE
    tiles = seq // t
    pw = u.shape[1]
    halo_blocks = t // POOL_HALO
    const = lambda i: (0, 0)
    return pl.pallas_call(
        functools.partial(_mix_out_kernel, tiles_per_seq=tiles),
        grid=(m // t,),
        in_specs=[
            pl.BlockSpec((t, o.shape[1]), lambda i: (i, 0)),
            pl.BlockSpec((t, pw), lambda i: (i, 0)),
            pl.BlockSpec((POOL_HALO, pw), lambda i: (jnp.maximum(i * halo_blocks - 1, 0), 0)),
            pl.BlockSpec((t, d), lambda i: (i, 0)),
            _resident(wp.shape, lambda i: (0, 0, 0)),
            _resident((1, pw), const),
            _resident(wo.shape, const),
            _resident((1, d), const),
            _resident((1, d), const),
        ],
        out_specs=[
            pl.BlockSpec((t, d), lambda i: (i, 0)),
            pl.BlockSpec((t, d), lambda i: (i, 0)),
        ],
        out_shape=[
            jax.ShapeDtypeStruct((m, d), F32),
            jax.ShapeDtypeStruct((m, d), BF16),
        ],
        compiler_params=_params(1),
        name="mix_out",
    )(o, u, u, x2, wp, ps, wo, gpost, gpre)


def _ffn_up_kernel(h_ref, wg_ref, wv_ref, cwg_ref, cwv_ref, cbg_ref, cbv_ref, a_ref,
                   carry_g_ref, carry_v_ref, *, tiles_per_seq):
    st = pl.program_id(1) % tiles_per_seq

    @pl.when(st == 0)
    def _():
        carry_g_ref[...] = jnp.zeros_like(carry_g_ref)
        carry_v_ref[...] = jnp.zeros_like(carry_v_ref)

    def conv(up, prev, cw_ref, cb_ref):
        ext = jnp.concatenate([prev, up], axis=0)
        x1 = pltpu.roll(ext, 1, 0)[CONV_HALO:, :]
        x2 = pltpu.roll(ext, 2, 0)[CONV_HALO:, :]
        return cb_ref[...] + x2 * cw_ref[0:1, :] + x1 * cw_ref[1:2, :] + up * cw_ref[2:3, :]

    prev_g = carry_g_ref[...]
    prev_v = carry_v_ref[...]
    rows = h_ref.shape[0]
    for r in range(0, rows, FFN_ROW_CHUNK):
        h = h_ref[r:r + FFN_ROW_CHUNK, :]
        up_g = jnp.dot(h, wg_ref[...], preferred_element_type=F32)
        up_v = jnp.dot(h, wv_ref[...], preferred_element_type=F32)
        gate = conv(up_g, prev_g, cwg_ref, cbg_ref)
        val = conv(up_v, prev_v, cwv_ref, cbv_ref)
        a_ref[r:r + FFN_ROW_CHUNK, :] = (jax.nn.gelu(gate, approximate=True) * val).astype(BF16)
        prev_g = up_g[FFN_ROW_CHUNK - CONV_HALO:, :]
        prev_v = up_v[FFN_ROW_CHUNK - CONV_HALO:, :]
    carry_g_ref[...] = prev_g
    carry_v_ref[...] = prev_v


def _ffn_up(h, w_up, conv_w, conv_b, seq):
    m, d = h.shape
    d_ff = w_up.shape[1] // 2
    tm, tn = FFN_ROW_TILE, FFN_COL_TILE
    nb = d_ff // tn
    tiles = seq // tm
    return pl.pallas_call(
        functools.partial(_ffn_up_kernel, tiles_per_seq=tiles),
        grid=(nb, m // tm),
        in_specs=[
            pl.BlockSpec((tm, d), lambda n, i: (i, 0)),
            pl.BlockSpec((d, tn), lambda n, i: (0, n)),
            pl.BlockSpec((d, tn), lambda n, i: (0, n + nb)),
            pl.BlockSpec((CONV_WIDTH, tn), lambda n, i: (0, n)),
            pl.BlockSpec((CONV_WIDTH, tn), lambda n, i: (0, n + nb)),
            pl.BlockSpec((1, tn), lambda n, i: (0, n)),
            pl.BlockSpec((1, tn), lambda n, i: (0, n + nb)),
        ],
        out_specs=pl.BlockSpec((tm, tn), lambda n, i: (i, n)),
        out_shape=jax.ShapeDtypeStruct((m, d_ff), BF16),
        scratch_shapes=[
            pltpu.VMEM((CONV_HALO, tn), F32),
            pltpu.VMEM((CONV_HALO, tn), F32),
        ],
        compiler_params=_params(2),
        name="ffn_up",
    )(h, w_up, w_up, conv_w, conv_w, conv_b, conv_b)


def _ffn_down_kernel(a_ref, w_ref, x_ref, g_ref, xo_ref):
    y = jnp.dot(a_ref[...], w_ref[...], preferred_element_type=F32)
    xo_ref[...] = x_ref[...] + y * _rms_scale(y) * g_ref[...]


def _ffn_down(a, w, x2, g):
    m, d = x2.shape
    d_ff = a.shape[1]
    tm = DOWN_ROW_TILE
    return pl.pallas_call(
        _ffn_down_kernel,
        grid=(m // tm,),
        in_specs=[
            pl.BlockSpec((tm, d_ff), lambda i: (i, 0)),
            _resident((d_ff, d), lambda i: (0, 0)),
            pl.BlockSpec((tm, d), lambda i: (i, 0)),
            _resident((1, d), lambda i: (0, 0)),
        ],
        out_specs=pl.BlockSpec((tm, d), lambda i: (i, 0)),
        out_shape=jax.ShapeDtypeStruct((m, d), F32),
        compiler_params=_params(1),
        name="ffn_down",
    )(a, w, x2, g)


def kernel(x, g_mix_pre, w_in, lam_q1, lam_k1, lam_q2, lam_k2, g_head, w_pool, pool_scale,
           w_out, g_mix_post, g_ffn_pre, w_up, conv_w, conv_b, w_down, g_ffn_post):
    batch, seq, d = x.shape
    depth = w_in.shape[0]
    assert seq % FFN_ROW_TILE == 0 and seq % SEQ_TILE == 0
    assert g_head.shape[1] == N_HEADS * D_V
    assert (w_up.shape[2] // 2) % FFN_COL_TILE == 0

    w_in_b = w_in.astype(BF16)
    w_pool_b = w_pool.astype(BF16)
    w_out_b = w_out.astype(BF16)
    w_up_b = w_up.astype(BF16)
    w_down_b = w_down.astype(BF16)

    x2 = x.reshape(batch * seq, d)
    for i in range(depth):
        lam_init = 0.8 - 0.6 * math.exp(-0.3 * i)
        qT, k, vT, u = _in_proj(x2, g_mix_pre[i][None], w_in_b[i], batch, seq)
        o = _attention(qT, k, vT, g_head[i].reshape(N_HEADS, D_V, 1),
                       lam_q1[i][None], lam_k1[i][None], lam_q2[i][None], lam_k2[i][None],
                       batch, seq, lam_init)
        x2, h = _mix_out(o, u, x2, w_pool_b[i], pool_scale[i][None], w_out_b[i],
                         g_mix_post[i][None], g_ffn_pre[i][None], seq)
        a = _ffn_up(h, w_up_b[i], conv_w[i], conv_b[i][None], seq)
        x2 = _ffn_down(a, w_down_b[i], x2, g_ffn_post[i][None])
    return x2.reshape(batch, seq, d)
```

```python
import functools
import math

import jax
import jax.numpy as jnp
from jax import lax
from jax.experimental import pallas as pl
from jax.experimental.pallas import tpu as pltpu

N_HEADS = 8
D_V = 128
D_QK = 64
POOL_WINDOWS = (2, 4, 8, 16)
POOL_HALO = 16
CONV_WIDTH = 3
CONV_HALO = 8
EPS = 1e-6
LOG2E = math.log2(math.e)
NEG_BIG = -1e30

SEQ_TILE = 512
ATTN_STRIP = 256
ATTN_SUM_ROWS = 16
ATTN_Q_TILES = 4
FFN_ROW_TILE = 1024
FFN_ROW_CHUNK = 256
FFN_COL_TILE = 512
DOWN_ROW_TILE = 256
VMEM_LIMIT_BYTES = 56 * 1024 * 1024

F32 = jnp.float32
BF16 = jnp.bfloat16


def _params(n_axes):
    return pltpu.CompilerParams(
        dimension_semantics=("arbitrary",) * n_axes,
        vmem_limit_bytes=VMEM_LIMIT_BYTES,
    )


def _resident(block_shape, index_map):
    return pl.BlockSpec(block_shape, index_map, pipeline_mode=pl.Buffered(1))


def _rms_scale(x):
    return lax.rsqrt(jnp.mean(x * x, axis=-1, keepdims=True) + EPS)


def _in_proj_kernel(x_ref, g_ref, w_ref, qT_ref, k_ref, vT_ref, u_ref, *, q_scale):
    x = x_ref[...]
    h = (x * _rms_scale(x) * g_ref[...]).astype(BF16)
    width = N_HEADS * D_V
    yq = jnp.dot(h, w_ref[:, 0:width], preferred_element_type=F32) * q_scale
    for hd in range(N_HEADS):
        qT_ref[0, hd, 0] = yq[:, hd * D_V:(hd + 1) * D_V].T.astype(BF16)
    yk = jnp.dot(h, w_ref[:, width:2 * width], preferred_element_type=F32)
    k_ref[...] = yk.astype(BF16)
    yv = jnp.dot(h, w_ref[:, 2 * width:3 * width], preferred_element_type=F32)
    for hd in range(N_HEADS):
        vT_ref[0, hd, 0] = yv[:, hd * D_V:(hd + 1) * D_V].T.astype(BF16)
    u_ref[...] = jnp.dot(h, w_ref[:, 3 * width:], preferred_element_type=F32)


def _in_proj(x2, g, w, batch, seq):
    m, d = x2.shape
    t = SEQ_TILE
    tiles = seq // t
    width = N_HEADS * D_V
    q_scale = (D_QK ** -0.5) * LOG2E
    t_shape = (batch, N_HEADS, tiles, D_V, t)
    t_spec = pl.BlockSpec((1, N_HEADS, 1, D_V, t), lambda i: (i // tiles, 0, i % tiles, 0, 0))
    return pl.pallas_call(
        functools.partial(_in_proj_kernel, q_scale=q_scale),
        grid=(m // t,),
        in_specs=[
            pl.BlockSpec((t, d), lambda i: (i, 0)),
            _resident((1, d), lambda i: (0, 0)),
            _resident((d, 4 * width), lambda i: (0, 0)),
        ],
        out_specs=[
            t_spec,
            pl.BlockSpec((t, width), lambda i: (i, 0)),
            t_spec,
            pl.BlockSpec((t, width), lambda i: (i, 0)),
        ],
        out_shape=[
            jax.ShapeDtypeStruct(t_shape, BF16),
            jax.ShapeDtypeStruct((m, width), BF16),
            jax.ShapeDtypeStruct(t_shape, BF16),
            jax.ShapeDtypeStruct((m, width), F32),
        ],
        compiler_params=_params(1),
        name="in_proj",
    )(x2, g, w)


def _split3(v):
    a = v.astype(BF16).astype(F32)
    b = (v - a).astype(BF16).astype(F32)
    c = (v - a - b).astype(BF16).astype(F32)
    return a, b, c


def _attn_kernel(lq1_ref, lk1_ref, lq2_ref, lk2_ref, gh_ref, qT_ref, k_ref, vT_ref, o_ref,
                 pf_ref, qf_ref, mask_ref, s_ref, acc_ref, stat_ref, *, lam_init):
    t = SEQ_TILE
    w = ATTN_STRIP
    nq = ATTN_Q_TILES
    tq = nq * t
    hd = pl.program_id(1)
    qi = pl.program_id(2)
    base = qi * nq

    def slope_row(n):
        return jnp.exp2(jnp.full((1, n), -8.0 / N_HEADS, F32) * (hd + 1).astype(F32)) * LOG2E

    slope = slope_row(tq)
    q_tile = lax.broadcasted_iota(jnp.int32, (1, tq), 1) // t

    @pl.when(qi == 0)
    def _():
        c = lax.broadcasted_iota(jnp.int32, (t, D_V), 0)
        lane = lax.broadcasted_iota(jnp.int32, (t, D_V), 1)
        c_hi = ((c >> 8) << 8).astype(F32)
        c_lo = (c & 255).astype(F32)
        pf = jnp.where(lane < 3, c_hi, jnp.where(lane < 6, c_lo, jnp.where(lane < 9, 1.0, 0.0)))
        pf_ref[...] = pf.astype(BF16)
        r = lax.broadcasted_iota(jnp.int32, (1, t), 1).astype(F32)
        sp = _split3(slope_row(t))
        gp = _split3(-slope_row(t) * r)
        row = lax.broadcasted_iota(jnp.int32, (D_V, t), 0)
        qf = jnp.zeros((D_V, t), F32)
        for i in range(3):
            qf = jnp.where((row == i) | (row == i + 3), sp[i], qf)
            qf = jnp.where(row == i + 6, gp[i], qf)
        qf_ref[...] = qf.astype(BF16)
        key = lax.broadcasted_iota(jnp.int32, (t, t), 0)
        qry = lax.broadcasted_iota(jnp.int32, (t, t), 1)
        mask_ref[...] = jnp.where(key > qry, NEG_BIG, 0.0)

    row = lax.broadcasted_iota(jnp.int32, (D_V, t), 0)
    qf = qf_ref[...]
    qw = ([], [])
    for b in range(nq):
        qT = qT_ref[0, 0, b]
        zero = jnp.zeros_like(qT)
        qw[0].append(jnp.concatenate([jnp.where(row < D_QK, qT, zero), qf], axis=0))
        qw[1].append(jnp.concatenate([jnp.where(row < D_QK, zero, qT), qf], axis=0))

    acc_ref[...] = jnp.zeros_like(acc_ref)
    stat_ref[0:2] = jnp.full((2, 1, tq), NEG_BIG, F32)

    def strips_from(lo):
        return [(b, h * w) for b in range(lo, nq) for h in range(t // w)]

    def logits(mp, j, stair):
        lo = 0 if stair is None else stair
        start = pl.multiple_of(j * t, t)
        kt = jnp.concatenate([k_ref[pl.ds(start, t), :], pf_ref[...]], axis=1)
        parts = []
        for b, c in strips_from(lo):
            sb = jnp.dot(kt, qw[mp][b][:, c:c + w], preferred_element_type=F32)
            if b == stair:
                sb = sb + mask_ref[:, c:c + w]
            s_ref[mp, :, b * t + c:b * t + c + w] = sb
            parts.append(jnp.max(sb, axis=0, keepdims=True))
        stat_ref[2 + mp, :, lo * t:tq] = jnp.concatenate(parts, axis=1)

    ones = jnp.ones((ATTN_SUM_ROWS, t), BF16)

    def values(mp, j, stair):
        lo = 0 if stair is None else stair
        live = slice(lo * t, tq)
        off = (slope * ((j - base - q_tile) * t).astype(F32))[:, live]
        vt = jnp.concatenate([vT_ref[0, 0, j], ones], axis=0)
        m_old = stat_ref[mp, :, live]
        m_new = jnp.maximum(m_old, stat_ref[2 + mp, :, live] + off)
        stat_ref[mp, :, live] = m_new
        mm = m_new - off
        alpha = jnp.exp2(m_old - m_new)
        for b, c in strips_from(lo):
            cols = slice(b * t + c, b * t + c + w)
            loc = slice((b - lo) * t + c, (b - lo) * t + c + w)
            e = jnp.exp2(s_ref[mp, :, cols] - mm[:, loc]).astype(BF16)
            pv = jnp.dot(vt, e, preferred_element_type=F32)
            acc_ref[mp, :, cols] = acc_ref[mp, :, cols] * alpha[:, loc] + pv

    def step(j, stair, stair_next):
        logits(1, j, stair)
        values(0, j, stair)
        logits(0, j + 1, stair_next)
        values(1, j, stair)

    @pl.when(qi == 0)
    def _():
        logits(0, 0, 0)

    @pl.when(qi > 0)
    def _():
        logits(0, 0, None)

    def body(j, carry):
        step(j, None, None)
        return carry

    lax.fori_loop(0, base - 1, body, 0)

    @pl.when(qi > 0)
    def _():
        step(base - 1, None, 0)

    for a in range(nq - 1):
        step(base + a, a, a + 1)
    last = base + nq - 1
    logits(1, last, nq - 1)
    values(0, last, nq - 1)
    values(1, last, nq - 1)

    lam = (jnp.exp(jnp.sum(lq1_ref[...] * lk1_ref[...], axis=-1, keepdims=True))
           - jnp.exp(jnp.sum(lq2_ref[...] * lk2_ref[...], axis=-1, keepdims=True))
           + lam_init)
    for b in range(nq):
        cols = slice(b * t, (b + 1) * t)
        l1 = acc_ref[0, D_V:D_V + 1, cols]
        l2 = acc_ref[1, D_V:D_V + 1, cols]
        o = acc_ref[0, 0:D_V, cols] * (1.0 / l1) - lam * (acc_ref[1, 0:D_V, cols] * (1.0 / l2))
        inv = lax.rsqrt(jnp.mean(o * o, axis=0, keepdims=True) + EPS)
        on = (o * inv) * gh_ref[0] * (1.0 - lam_init)
        o_ref[cols, :] = on.T.astype(BF16)


def _attention(qT, k, vT, gh, lq1, lk1, lq2, lk2, batch, seq, lam_init):
    t = SEQ_TILE
    nq = ATTN_Q_TILES
    tq = nq * t
    tiles = seq // t
    blocks = seq // tq
    m, width = k.shape
    small = _resident((1, D_QK), lambda b, h, q: (0, 0))
    return pl.pallas_call(
        functools.partial(_attn_kernel, lam_init=lam_init),
        grid=(batch, N_HEADS, blocks),
        in_specs=[
            small, small, small, small,
            pl.BlockSpec((1, D_V, 1), lambda b, h, q: (h, 0, 0)),
            pl.BlockSpec((1, 1, nq, D_V, t), lambda b, h, q: (b, h, q, 0, 0)),
            pl.BlockSpec((seq, D_V), lambda b, h, q: (b, h)),
            pl.BlockSpec((1, 1, tiles, D_V, t), lambda b, h, q: (b, h, 0, 0, 0)),
        ],
        out_specs=pl.BlockSpec((tq, D_V), lambda b, h, q: (b * blocks + q, h)),
        out_shape=jax.ShapeDtypeStruct((m, width), BF16),
        scratch_shapes=[
            pltpu.VMEM((t, D_V), BF16),
            pltpu.VMEM((D_V, t), BF16),
            pltpu.VMEM((t, t), F32),
            pltpu.VMEM((2, t, tq), F32),
            pltpu.VMEM((2, D_V + ATTN_SUM_ROWS, tq), F32),
            pltpu.VMEM((4, 1, tq), F32),
        ],
        compiler_params=_params(3),
        name="attn",
    )(lq1, lk1, lq2, lk2, gh, qT, k, vT)


def _mix_out_kernel(o_ref, u_ref, uh_ref, x_ref, wp_ref, ps_ref, wo_ref, gpost_ref, gpre_ref,
                    xo_ref, h_ref, *, tiles_per_seq):
    t = SEQ_TILE
    st = pl.program_id(0) % tiles_per_seq
    u = u_ref[...]
    halo = jnp.where(st == 0, 0.0, uh_ref[...])
    ext = jnp.concatenate([halo, u], axis=0)
    pos = (st * t + lax.broadcasted_iota(jnp.int32, (t, 1), 0) + 1).astype(F32)
    group = u.shape[1] // len(POOL_WINDOWS)
    attn_w = o_ref.shape[1]

    mix = jnp.dot(o_ref[...], wo_ref[0:attn_w, :], preferred_element_type=F32)
    for g, w in enumerate(POOL_WINDOWS):
        cols = slice(g * group, (g + 1) * group)
        s = ext[:, cols]
        span = 1
        while span < w:
            s = s + pltpu.roll(s, span, 0)
            span *= 2
        pm = s[POOL_HALO:, :] / jnp.minimum(pos, float(w)) - u[:, cols]
        pw = jnp.dot(pm.astype(BF16), wp_ref[g], preferred_element_type=F32) * ps_ref[:, cols]
        mix = mix + jnp.dot(pw.astype(BF16), wo_ref[attn_w + g * group:attn_w + (g + 1) * group, :],
                            preferred_element_type=F32)

    xn = x_ref[...] + mix * _rms_scale(mix) * gpost_ref[...]
    xo_ref[...] = xn
    h_ref[...] = (xn * _rms_scale(xn) * gpre_ref[...]).astype(BF16)


def _mix_out(o, u, x2, wp, ps, wo, gpost, gpre, seq):
    m, d = x2.shape
    t = SEQ_TILE
    tiles = seq // t
    pw = u.shape[1]
    halo_blocks = t // POOL_HALO
    const = lambda i: (0, 0)
    return pl.pallas_call(
        functools.partial(_mix_out_kernel, tiles_per_seq=tiles),
        grid=(m // t,),
        in_specs=[
            pl.BlockSpec((t, o.shape[1]), lambda i: (i, 0)),
            pl.BlockSpec((t, pw), lambda i: (i, 0)),
            pl.BlockSpec((POOL_HALO, pw), lambda i: (jnp.maximum(i * halo_blocks - 1, 0), 0)),
            pl.BlockSpec((t, d), lambda i: (i, 0)),
            _resident(wp.shape, lambda i: (0, 0, 0)),
            _resident((1, pw), const),
            _resident(wo.shape, const),
            _resident((1, d), const),
            _resident((1, d), const),
        ],
        out_specs=[
            pl.BlockSpec((t, d), lambda i: (i, 0)),
            pl.BlockSpec((t, d), lambda i: (i, 0)),
        ],
        out_shape=[
            jax.ShapeDtypeStruct((m, d), F32),
            jax.ShapeDtypeStruct((m, d), BF16),
        ],
        compiler_params=_params(1),
        name="mix_out",
    )(o, u, u, x2, wp, ps, wo, gpost, gpre)


def _ffn_up_kernel(h_ref, wg_ref, wv_ref, cwg_ref, cwv_ref, cbg_ref, cbv_ref, a_ref,
                   carry_g_ref, carry_v_ref, *, tiles_per_seq):
    st = pl.program_id(1) % tiles_per_seq

    @pl.when(st == 0)
    def _():
        carry_g_ref[...] = jnp.zeros_like(carry_g_ref)
        carry_v_ref[...] = jnp.zeros_like(carry_v_ref)

    def conv(up, prev, cw_ref, cb_ref):
        ext = jnp.concatenate([prev, up], axis=0)
        x1 = pltpu.roll(ext, 1, 0)[CONV_HALO:, :]
        x2 = pltpu.roll(ext, 2, 0)[CONV_HALO:, :]
        return cb_ref[...] + x2 * cw_ref[0:1, :] + x1 * cw_ref[1:2, :] + up * cw_ref[2:3, :]

    prev_g = carry_g_ref[...]
    prev_v = carry_v_ref[...]
    rows = h_ref.shape[0]
    for r in range(0, rows, FFN_ROW_CHUNK):
        h = h_ref[r:r + FFN_ROW_CHUNK, :]
        up_g = jnp.dot(h, wg_ref[...], preferred_element_type=F32)
        up_v = jnp.dot(h, wv_ref[...], preferred_element_type=F32)
        gate = conv(up_g, prev_g, cwg_ref, cbg_ref)
        val = conv(up_v, prev_v, cwv_ref, cbv_ref)
        a_ref[r:r + FFN_ROW_CHUNK, :] = (jax.nn.gelu(gate, approximate=True) * val).astype(BF16)
        prev_g = up_g[FFN_ROW_CHUNK - CONV_HALO:, :]
        prev_v = up_v[FFN_ROW_CHUNK - CONV_HALO:, :]
    carry_g_ref[...] = prev_g
    carry_v_ref[...] = prev_v


def _ffn_up(h, w_up, conv_w, conv_b, seq):
    m, d = h.shape
    d_ff = w_up.shape[1] // 2
    tm, tn = FFN_ROW_TILE, FFN_COL_TILE
    nb = d_ff // tn
    tiles = seq // tm
    return pl.pallas_call(
        functools.partial(_ffn_up_kernel, tiles_per_seq=tiles),
        grid=(nb, m // tm),
        in_specs=[
            pl.BlockSpec((tm, d), lambda n, i: (i, 0)),
            pl.BlockSpec((d, tn), lambda n, i: (0, n)),
            pl.BlockSpec((d, tn), lambda n, i: (0, n + nb)),
            pl.BlockSpec((CONV_WIDTH, tn), lambda n, i: (0, n)),
            pl.BlockSpec((CONV_WIDTH, tn), lambda n, i: (0, n + nb)),
            pl.BlockSpec((1, tn), lambda n, i: (0, n)),
            pl.BlockSpec((1, tn), lambda n, i: (0, n + nb)),
        ],
        out_specs=pl.BlockSpec((tm, tn), lambda n, i: (i, n)),
        out_shape=jax.ShapeDtypeStruct((m, d_ff), BF16),
        scratch_shapes=[
            pltpu.VMEM((CONV_HALO, tn), F32),
            pltpu.VMEM((CONV_HALO, tn), F32),
        ],
        compiler_params=_params(2),
        name="ffn_up",
    )(h, w_up, w_up, conv_w, conv_w, conv_b, conv_b)


def _ffn_down_kernel(a_ref, w_ref, x_ref, g_ref, xo_ref):
    y = jnp.dot(a_ref[...], w_ref[...], preferred_element_type=F32)
    xo_ref[...] = x_ref[...] + y * _rms_scale(y) * g_ref[...]


def _ffn_down(a, w, x2, g):
    m, d = x2.shape
    d_ff = a.shape[1]
    tm = DOWN_ROW_TILE
    return pl.pallas_call(
        _ffn_down_kernel,
        grid=(m // tm,),
        in_specs=[
            pl.BlockSpec((tm, d_ff), lambda i: (i, 0)),
            _resident((d_ff, d), lambda i: (0, 0)),
            pl.BlockSpec((tm, d), lambda i: (i, 0)),
            _resident((1, d), lambda i: (0, 0)),
        ],
        out_specs=pl.BlockSpec((tm, d), lambda i: (i, 0)),
        out_shape=jax.ShapeDtypeStruct((m, d), F32),
        compiler_params=_params(1),
        name="ffn_down",
    )(a, w, x2, g)


def kernel(x, g_mix_pre, w_in, lam_q1, lam_k1, lam_q2, lam_k2, g_head, w_pool, pool_scale,
           w_out, g_mix_post, g_ffn_pre, w_up, conv_w, conv_b, w_down, g_ffn_post):
    batch, seq, d = x.shape
    depth = w_in.shape[0]
    assert seq % FFN_ROW_TILE == 0 and seq % (SEQ_TILE * ATTN_Q_TILES) == 0
    assert g_head.shape[1] == N_HEADS * D_V
    assert (w_up.shape[2] // 2) % FFN_COL_TILE == 0

    w_in_b = w_in.astype(BF16)
    w_pool_b = w_pool.astype(BF16)
    w_out_b = w_out.astype(BF16)
    w_up_b = w_up.astype(BF16)
    w_down_b = w_down.astype(BF16)

    x2 = x.reshape(batch * seq, d)
    for i in range(depth):
        lam_init = 0.8 - 0.6 * math.exp(-0.3 * i)
        qT, k, vT, u = _in_proj(x2, g_mix_pre[i][None], w_in_b[i], batch, seq)
        o = _attention(qT, k, vT, g_head[i].reshape(N_HEADS, D_V, 1),
                       lam_q1[i][None], lam_k1[i][None], lam_q2[i][None], lam_k2[i][None],
                       batch, seq, lam_init)
        x2, h = _mix_out(o, u, x2, w_pool_b[i], pool_scale[i][None], w_out_b[i],
                         g_mix_post[i][None], g_ffn_pre[i][None], seq)
        a = _ffn_up(h, w_up_b[i], conv_w[i], conv_b[i][None], seq)
        x2 = _ffn_down(a, w_down_b[i], x2, g_ffn_post[i][None])
    return x2.reshape(batch, seq, d)
```

```python
import functools
import math

import jax
import jax.numpy as jnp
from jax import lax
from jax.experimental import pallas as pl
from jax.experimental.pallas import tpu as pltpu

N_HEADS = 8
D_V = 128
D_QK = 64
POOL_WINDOWS = (2, 4, 8, 16)
POOL_HALO = 16
CONV_WIDTH = 3
CONV_HALO = 8
EPS = 1e-6
LOG2E = math.log2(math.e)
NEG_BIG = -1e30

SEQ_TILE = 512
ATTN_STRIP = 256
ATTN_SUM_ROWS = 16
ATTN_Q_TILES = 4
MIX_ROW_CHUNK = 256
FFN_ROW_TILE = 1024
FFN_ROW_CHUNK = 256
FFN_COL_TILE = 512
DOWN_ROW_TILE = 256
VMEM_LIMIT_BYTES = 56 * 1024 * 1024

F32 = jnp.float32
BF16 = jnp.bfloat16


def _params(n_axes):
    return pltpu.CompilerParams(
        dimension_semantics=("arbitrary",) * n_axes,
        vmem_limit_bytes=VMEM_LIMIT_BYTES,
    )


def _layer(layer, block_shape, index_map, resident=False):
    return pl.BlockSpec((None,) + tuple(block_shape), lambda *ids: (layer,) + tuple(index_map(*ids)),
                        pipeline_mode=pl.Buffered(1) if resident else None)


def _rms_scale(x):
    return lax.rsqrt(jnp.mean(x * x, axis=-1, keepdims=True) + EPS)


def _in_proj_kernel(x_ref, g_ref, w_ref, qT_ref, k_ref, vT_ref, u_ref, *, q_scale):
    x = x_ref[...]
    h = (x * _rms_scale(x) * g_ref[...]).astype(BF16)
    width = N_HEADS * D_V
    yq = jnp.dot(h, w_ref[:, 0:width], preferred_element_type=F32) * q_scale
    for hd in range(N_HEADS):
        qT_ref[0, hd, 0] = yq[:, hd * D_V:(hd + 1) * D_V].T.astype(BF16)
    yk = jnp.dot(h, w_ref[:, width:2 * width], preferred_element_type=F32)
    k_ref[...] = yk.astype(BF16)
    yv = jnp.dot(h, w_ref[:, 2 * width:3 * width], preferred_element_type=F32)
    for hd in range(N_HEADS):
        vT_ref[0, hd, 0] = yv[:, hd * D_V:(hd + 1) * D_V].T.astype(BF16)
    u_ref[...] = jnp.dot(h, w_ref[:, 3 * width:], preferred_element_type=F32)


def _in_proj(x2, g, w, layer, batch, seq):
    m, d = x2.shape
    t = SEQ_TILE
    tiles = seq // t
    width = N_HEADS * D_V
    q_scale = (D_QK ** -0.5) * LOG2E
    t_shape = (batch, N_HEADS, tiles, D_V, t)
    t_spec = pl.BlockSpec((1, N_HEADS, 1, D_V, t), lambda i: (i // tiles, 0, i % tiles, 0, 0))
    return pl.pallas_call(
        functools.partial(_in_proj_kernel, q_scale=q_scale),
        grid=(m // t,),
        in_specs=[
            pl.BlockSpec((t, d), lambda i: (i, 0)),
            _layer(layer, (1, d), lambda i: (0, 0), resident=True),
            _layer(layer, (d, 4 * width), lambda i: (0, 0), resident=True),
        ],
        out_specs=[
            t_spec,
            pl.BlockSpec((t, width), lambda i: (i, 0)),
            t_spec,
            pl.BlockSpec((t, width), lambda i: (i, 0)),
        ],
        out_shape=[
            jax.ShapeDtypeStruct(t_shape, BF16),
            jax.ShapeDtypeStruct((m, width), BF16),
            jax.ShapeDtypeStruct(t_shape, BF16),
            jax.ShapeDtypeStruct((m, width), F32),
        ],
        compiler_params=_params(1),
        name="in_proj",
    )(x2, g, w)


def _split3(v):
    a = v.astype(BF16).astype(F32)
    b = (v - a).astype(BF16).astype(F32)
    c = (v - a - b).astype(BF16).astype(F32)
    return a, b, c


def _attn_kernel(lq1_ref, lk1_ref, lq2_ref, lk2_ref, gh_ref, qT_ref, k_ref, vT_ref, o_ref,
                 pf_ref, qf_ref, mask_ref, s_ref, acc_ref, stat_ref, *, lam_init):
    t = SEQ_TILE
    w = ATTN_STRIP
    nq = ATTN_Q_TILES
    tq = nq * t
    hd = pl.program_id(1)
    qi = pl.program_id(2)
    base = qi * nq

    def slope_row(n):
        return jnp.exp2(jnp.full((1, n), -8.0 / N_HEADS, F32) * (hd + 1).astype(F32)) * LOG2E

    slope = slope_row(tq)
    q_tile = lax.broadcasted_iota(jnp.int32, (1, tq), 1) // t

    @pl.when(qi == 0)
    def _():
        c = lax.broadcasted_iota(jnp.int32, (t, D_V), 0)
        lane = lax.broadcasted_iota(jnp.int32, (t, D_V), 1)
        c_hi = ((c >> 8) << 8).astype(F32)
        c_lo = (c & 255).astype(F32)
        pf = jnp.where(lane < 3, c_hi, jnp.where(lane < 6, c_lo, jnp.where(lane < 9, 1.0, 0.0)))
        pf_ref[...] = pf.astype(BF16)
        r = lax.broadcasted_iota(jnp.int32, (1, t), 1).astype(F32)
        sp = _split3(slope_row(t))
        gp = _split3(-slope_row(t) * r)
        row = lax.broadcasted_iota(jnp.int32, (D_V, t), 0)
        qf = jnp.zeros((D_V, t), F32)
        for i in range(3):
            qf = jnp.where((row == i) | (row == i + 3), sp[i], qf)
            qf = jnp.where(row == i + 6, gp[i], qf)
        qf_ref[...] = qf.astype(BF16)
        key = lax.broadcasted_iota(jnp.int32, (t, t), 0)
        qry = lax.broadcasted_iota(jnp.int32, (t, t), 1)
        mask_ref[...] = jnp.where(key > qry, NEG_BIG, 0.0)

    row = lax.broadcasted_iota(jnp.int32, (D_V, t), 0)
    qf = qf_ref[...]
    qw = ([], [])
    for b in range(nq):
        qT = qT_ref[0, 0, b]
        zero = jnp.zeros_like(qT)
        qw[0].append(jnp.concatenate([jnp.where(row < D_QK, qT, zero), qf], axis=0))
        qw[1].append(jnp.concatenate([jnp.where(row < D_QK, zero, qT), qf], axis=0))

    acc_ref[...] = jnp.zeros_like(acc_ref)
    stat_ref[0:2] = jnp.full((2, 1, tq), NEG_BIG, F32)

    def strips_from(lo):
        return [(b, h * w) for b in range(lo, nq) for h in range(t // w)]

    def logits(mp, j, stair):
        lo = 0 if stair is None else stair
        start = pl.multiple_of(j * t, t)
        kt = jnp.concatenate([k_ref[pl.ds(start, t), :], pf_ref[...]], axis=1)
        parts = []
        for b, c in strips_from(lo):
            sb = jnp.dot(kt, qw[mp][b][:, c:c + w], preferred_element_type=F32)
            if b == stair:
                sb = sb + mask_ref[:, c:c + w]
            s_ref[mp, :, b * t + c:b * t + c + w] = sb
            parts.append(jnp.max(sb, axis=0, keepdims=True))
        stat_ref[2 + mp, :, lo * t:tq] = jnp.concatenate(parts, axis=1)

    ones = jnp.ones((ATTN_SUM_ROWS, t), BF16)

    def values(mp, j, stair):
        lo = 0 if stair is None else stair
        live = slice(lo * t, tq)
        off = (slope * ((j - base - q_tile) * t).astype(F32))[:, live]
        vt = jnp.concatenate([vT_ref[0, 0, j], ones], axis=0)
        m_old = stat_ref[mp, :, live]
        m_new = jnp.maximum(m_old, stat_ref[2 + mp, :, live] + off)
        stat_ref[mp, :, live] = m_new
        mm = m_new - off
        alpha = jnp.exp2(m_old - m_new)
        for b, c in strips_from(lo):
            cols = slice(b * t + c, b * t + c + w)
            loc = slice((b - lo) * t + c, (b - lo) * t + c + w)
            e = jnp.exp2(s_ref[mp, :, cols] - mm[:, loc]).astype(BF16)
            pv = jnp.dot(vt, e, preferred_element_type=F32)
            acc_ref[mp, :, cols] = acc_ref[mp, :, cols] * alpha[:, loc] + pv

    def step(j, stair, stair_next):
        logits(1, j, stair)
        values(0, j, stair)
        logits(0, j + 1, stair_next)
        values(1, j, stair)

    @pl.when(qi == 0)
    def _():
        logits(0, 0, 0)

    @pl.when(qi > 0)
    def _():
        logits(0, 0, None)

    def body(p, carry):
        step(2 * p + 1, None, None)
        step(2 * p + 2, None, None)
        return carry

    @pl.when(qi > 0)
    def _():
        step(0, None, None)

    lax.fori_loop(0, (base - 2) // 2, body, 0)

    @pl.when(qi > 0)
    def _():
        step(base - 1, None, 0)

    for a in range(nq - 1):
        step(base + a, a, a + 1)
    last = base + nq - 1
    logits(1, last, nq - 1)
    values(0, last, nq - 1)
    values(1, last, nq - 1)

    lam = (jnp.exp(jnp.sum(lq1_ref[...] * lk1_ref[...], axis=-1, keepdims=True))
           - jnp.exp(jnp.sum(lq2_ref[...] * lk2_ref[...], axis=-1, keepdims=True))
           + lam_init)
    for b in range(nq):
        cols = slice(b * t, (b + 1) * t)
        l1 = acc_ref[0, D_V:D_V + 1, cols]
        l2 = acc_ref[1, D_V:D_V + 1, cols]
        o = acc_ref[0, 0:D_V, cols] * (1.0 / l1) - lam * (acc_ref[1, 0:D_V, cols] * (1.0 / l2))
        inv = lax.rsqrt(jnp.mean(o * o, axis=0, keepdims=True) + EPS)
        on = (o * inv) * gh_ref[0] * (1.0 - lam_init)
        o_ref[cols, :] = on.T.astype(BF16)


def _attention(qT, k, vT, gh, lq1, lk1, lq2, lk2, layer, batch, seq, lam_init):
    assert ATTN_Q_TILES % 2 == 0
    t = SEQ_TILE
    nq = ATTN_Q_TILES
    tq = nq * t
    tiles = seq // t
    blocks = seq // tq
    m, width = k.shape
    small = _layer(layer, (1, D_QK), lambda b, h, q: (0, 0), resident=True)
    return pl.pallas_call(
        functools.partial(_attn_kernel, lam_init=lam_init),
        grid=(batch, N_HEADS, blocks),
        in_specs=[
            small, small, small, small,
            _layer(layer, (1, D_V, 1), lambda b, h, q: (h, 0, 0)),
            pl.BlockSpec((1, 1, nq, D_V, t), lambda b, h, q: (b, h, q, 0, 0)),
            pl.BlockSpec((seq, D_V), lambda b, h, q: (b, h)),
            pl.BlockSpec((1, 1, tiles, D_V, t), lambda b, h, q: (b, h, 0, 0, 0)),
        ],
        out_specs=pl.BlockSpec((tq, D_V), lambda b, h, q: (b * blocks + q, h)),
        out_shape=jax.ShapeDtypeStruct((m, width), BF16),
        scratch_shapes=[
            pltpu.VMEM((t, D_V), BF16),
            pltpu.VMEM((D_V, t), BF16),
            pltpu.VMEM((t, t), F32),
            pltpu.VMEM((2, t, tq), F32),
            pltpu.VMEM((2, D_V + ATTN_SUM_ROWS, tq), F32),
            pltpu.VMEM((4, 1, tq), F32),
        ],
        compiler_params=_params(3),
        name="attn",
    )(lq1, lk1, lq2, lk2, gh, qT, k, vT)


def _mix_out_kernel(o_ref, u_ref, uh_ref, x_ref, wp_ref, ps_ref, wo_ref, gpost_ref, gpre_ref,
                    xo_ref, h_ref, *, tiles_per_seq):
    t = SEQ_TILE
    n = MIX_ROW_CHUNK
    st = pl.program_id(0) % tiles_per_seq
    group = u_ref.shape[1] // len(POOL_WINDOWS)
    attn_w = o_ref.shape[1]
    halo = jnp.where(st == 0, 0.0, uh_ref[...])

    for r in range(0, t, n):
        u = u_ref[r:r + n, :]
        prev = halo if r == 0 else u_ref[r - POOL_HALO:r, :]
        ext = jnp.concatenate([prev, u], axis=0)
        pos = (st * t + r + lax.broadcasted_iota(jnp.int32, (n, 1), 0) + 1).astype(F32)
        mix = jnp.dot(o_ref[r:r + n, :], wo_ref[0:attn_w, :], preferred_element_type=F32)
        pooled = []
        for g, w in enumerate(POOL_WINDOWS):
            cols = slice(g * group, (g + 1) * group)
            s = ext[:, cols]
            span = 1
            while span < w:
                s = s + pltpu.roll(s, span, 0)
                span *= 2
            pm = s[POOL_HALO:, :] / jnp.minimum(pos, float(w)) - u[:, cols]
            pw = jnp.dot(pm.astype(BF16), wp_ref[g], preferred_element_type=F32) * ps_ref[:, cols]
            pooled.append(pw.astype(BF16))
        mix = mix + jnp.dot(jnp.concatenate(pooled, axis=1), wo_ref[attn_w:, :], preferred_element_type=F32)
        xn = x_ref[r:r + n, :] + mix * _rms_scale(mix) * gpost_ref[...]
        xo_ref[r:r + n, :] = xn
        h_ref[r:r + n, :] = (xn * _rms_scale(xn) * gpre_ref[...]).astype(BF16)


def _mix_out(o, u, x2, wp, ps, wo, gpost, gpre, layer, seq):
    m, d = x2.shape
    t = SEQ_TILE
    tiles = seq // t
    pw = u.shape[1]
    halo_blocks = t // POOL_HALO
    const = lambda i: (0, 0)
    return pl.pallas_call(
        functools.partial(_mix_out_kernel, tiles_per_seq=tiles),
        grid=(m // t,),
        in_specs=[
            pl.BlockSpec((t, o.shape[1]), lambda i: (i, 0)),
            pl.BlockSpec((t, pw), lambda i: (i, 0)),
            pl.BlockSpec((POOL_HALO, pw), lambda i: (jnp.maximum(i * halo_blocks - 1, 0), 0)),
            pl.BlockSpec((t, d), lambda i: (i, 0)),
            _layer(layer, wp.shape[1:], lambda i: (0, 0, 0), resident=True),
            _layer(layer, (1, pw), const, resident=True),
            _layer(layer, wo.shape[1:], const, resident=True),
            _layer(layer, (1, d), const, resident=True),
            _layer(layer, (1, d), const, resident=True),
        ],
        out_specs=[
            pl.BlockSpec((t, d), lambda i: (i, 0)),
            pl.BlockSpec((t, d), lambda i: (i, 0)),
        ],
        out_shape=[
            jax.ShapeDtypeStruct((m, d), F32),
            jax.ShapeDtypeStruct((m, d), BF16),
        ],
        compiler_params=_params(1),
        name="mix_out",
    )(o, u, u, x2, wp, ps, wo, gpost, gpre)


def _ffn_up_kernel(h_ref, wg_ref, wv_ref, cwg_ref, cwv_ref, cbg_ref, cbv_ref, a_ref,
                   carry_g_ref, carry_v_ref, *, tiles_per_seq):
    st = pl.program_id(1) % tiles_per_seq

    @pl.when(st == 0)
    def _():
        carry_g_ref[...] = jnp.zeros_like(carry_g_ref)
        carry_v_ref[...] = jnp.zeros_like(carry_v_ref)

    def conv(up, prev, cw_ref, cb_ref):
        ext = jnp.concatenate([prev, up], axis=0)
        x1 = pltpu.roll(ext, 1, 0)[CONV_HALO:, :]
        x2 = pltpu.roll(ext, 2, 0)[CONV_HALO:, :]
        return cb_ref[...] + x2 * cw_ref[0:1, :] + x1 * cw_ref[1:2, :] + up * cw_ref[2:3, :]

    prev_g = carry_g_ref[...]
    prev_v = carry_v_ref[...]
    rows = h_ref.shape[0]
    for r in range(0, rows, FFN_ROW_CHUNK):
        h = h_ref[r:r + FFN_ROW_CHUNK, :]
        up_g = jnp.dot(h, wg_ref[...], preferred_element_type=F32)
        up_v = jnp.dot(h, wv_ref[...], preferred_element_type=F32)
        gate = conv(up_g, prev_g, cwg_ref, cbg_ref)
        val = conv(up_v, prev_v, cwv_ref, cbv_ref)
        a_ref[r:r + FFN_ROW_CHUNK, :] = (jax.nn.gelu(gate, approximate=True) * val).astype(BF16)
        prev_g = up_g[FFN_ROW_CHUNK - CONV_HALO:, :]
        prev_v = up_v[FFN_ROW_CHUNK - CONV_HALO:, :]
    carry_g_ref[...] = prev_g
    carry_v_ref[...] = prev_v


def _ffn_up(h, w_up, conv_w, conv_b, layer, seq):
    m, d = h.shape
    d_ff = w_up.shape[2] // 2
    tm, tn = FFN_ROW_TILE, FFN_COL_TILE
    nb = d_ff // tn
    tiles = seq // tm
    return pl.pallas_call(
        functools.partial(_ffn_up_kernel, tiles_per_seq=tiles),
        grid=(nb, m // tm),
        in_specs=[
            pl.BlockSpec((tm, d), lambda n, i: (i, 0)),
            _layer(layer, (d, tn), lambda n, i: (0, n)),
            _layer(layer, (d, tn), lambda n, i: (0, n + nb)),
            _layer(layer, (CONV_WIDTH, tn), lambda n, i: (0, n)),
            _layer(layer, (CONV_WIDTH, tn), lambda n, i: (0, n + nb)),
            _layer(layer, (1, tn), lambda n, i: (0, n)),
            _layer(layer, (1, tn), lambda n, i: (0, n + nb)),
        ],
        out_specs=pl.BlockSpec((tm, tn), lambda n, i: (i, n)),
        out_shape=jax.ShapeDtypeStruct((m, d_ff), BF16),
        scratch_shapes=[
            pltpu.VMEM((CONV_HALO, tn), F32),
            pltpu.VMEM((CONV_HALO, tn), F32),
        ],
        compiler_params=_params(2),
        name="ffn_up",
    )(h, w_up, w_up, conv_w, conv_w, conv_b, conv_b)


def _ffn_down_kernel(a_ref, w_ref, x_ref, g_ref, xo_ref):
    y = jnp.dot(a_ref[...], w_ref[...], preferred_element_type=F32)
    xo_ref[...] = x_ref[...] + y * _rms_scale(y) * g_ref[...]


def _ffn_down(a, w, x2, g, layer):
    m, d = x2.shape
    d_ff = a.shape[1]
    tm = DOWN_ROW_TILE
    return pl.pallas_call(
        _ffn_down_kernel,
        grid=(m // tm,),
        in_specs=[
            pl.BlockSpec((tm, d_ff), lambda i: (i, 0)),
            _layer(layer, (d_ff, d), lambda i: (0, 0), resident=True),
            pl.BlockSpec((tm, d), lambda i: (i, 0)),
            _layer(layer, (1, d), lambda i: (0, 0), resident=True),
        ],
        out_specs=pl.BlockSpec((tm, d), lambda i: (i, 0)),
        out_shape=jax.ShapeDtypeStruct((m, d), F32),
        compiler_params=_params(1),
        name="ffn_down",
    )(a, w, x2, g)


def kernel(x, g_mix_pre, w_in, lam_q1, lam_k1, lam_q2, lam_k2, g_head, w_pool, pool_scale,
           w_out, g_mix_post, g_ffn_pre, w_up, conv_w, conv_b, w_down, g_ffn_post):
    batch, seq, d = x.shape
    depth = w_in.shape[0]
    assert seq % FFN_ROW_TILE == 0 and seq % (SEQ_TILE * ATTN_Q_TILES) == 0
    assert g_head.shape[1] == N_HEADS * D_V
    assert (w_up.shape[2] // 2) % FFN_COL_TILE == 0

    w_in_b = w_in.astype(BF16)
    w_pool_b = w_pool.astype(BF16)
    w_out_b = w_out.astype(BF16)
    w_up_b = w_up.astype(BF16)
    w_down_b = w_down.astype(BF16)
    rows = lambda p: p[:, None, :]
    g_mix_pre, g_mix_post, g_ffn_pre, g_ffn_post = map(rows, (g_mix_pre, g_mix_post, g_ffn_pre, g_ffn_post))
    lam_q1, lam_k1, lam_q2, lam_k2 = map(rows, (lam_q1, lam_k1, lam_q2, lam_k2))
    pool_scale, conv_b = rows(pool_scale), rows(conv_b)
    g_head = g_head.reshape(depth, N_HEADS, D_V, 1)

    x2 = x.reshape(batch * seq, d)
    for i in range(depth):
        lam_init = 0.8 - 0.6 * math.exp(-0.3 * i)
        qT, k, vT, u = _in_proj(x2, g_mix_pre, w_in_b, i, batch, seq)
        o = _attention(qT, k, vT, g_head, lam_q1, lam_k1, lam_q2, lam_k2, i, batch, seq, lam_init)
        x2, h = _mix_out(o, u, x2, w_pool_b, pool_scale, w_out_b, g_mix_post, g_ffn_pre, i, seq)
        a = _ffn_up(h, w_up_b, conv_w, conv_b, i, seq)
        x2 = _ffn_down(a, w_down_b, x2, g_ffn_post, i)
    return x2.reshape(batch, seq, d)
```

```python
import functools
import math

import jax
import jax.numpy as jnp
from jax import lax
from jax.experimental import pallas as pl
from jax.experimental.pallas import tpu as pltpu

N_HEADS = 8
D_V = 128
D_QK = 64
POOL_WINDOWS = (2, 4, 8, 16)
POOL_HALO = 16
CONV_WIDTH = 3
CONV_HALO = 8
EPS = 1e-6
LOG2E = math.log2(math.e)
NEG_BIG = -1e30

SEQ_TILE = 512
ATTN_STRIP = 256
ATTN_SUM_ROWS = 16
ATTN_Q_TILES = 4
MIX_ROW_CHUNK = 256
FFN_ROW_TILE = 2048
FFN_ROW_CHUNK = 1024
FFN_COL_TILE = 512
DOWN_ROW_TILE = 256
VMEM_LIMIT_BYTES = 56 * 1024 * 1024

F32 = jnp.float32
BF16 = jnp.bfloat16


def _params(n_axes):
    return pltpu.CompilerParams(
        dimension_semantics=("arbitrary",) * n_axes,
        vmem_limit_bytes=VMEM_LIMIT_BYTES,
    )


def _layer(layer, block_shape, index_map, resident=False):
    return pl.BlockSpec((None,) + tuple(block_shape), lambda *ids: (layer,) + tuple(index_map(*ids)),
                        pipeline_mode=pl.Buffered(1) if resident else None)


def _rms_scale(x):
    return lax.rsqrt(jnp.mean(x * x, axis=-1, keepdims=True) + EPS)


def _in_proj_kernel(x_ref, g_ref, w_ref, qT_ref, k_ref, vT_ref, u_ref, *, q_scale):
    x = x_ref[...]
    h = (x * _rms_scale(x) * g_ref[...]).astype(BF16)
    width = N_HEADS * D_V
    yq = jnp.dot(h, w_ref[:, 0:width], preferred_element_type=F32) * q_scale
    for hd in range(N_HEADS):
        qT_ref[0, hd, 0] = yq[:, hd * D_V:(hd + 1) * D_V].T.astype(BF16)
    yk = jnp.dot(h, w_ref[:, width:2 * width], preferred_element_type=F32)
    k_ref[...] = yk.astype(BF16)
    yv = jnp.dot(h, w_ref[:, 2 * width:3 * width], preferred_element_type=F32)
    for hd in range(N_HEADS):
        vT_ref[0, hd, 0] = yv[:, hd * D_V:(hd + 1) * D_V].T.astype(BF16)
    u_ref[...] = jnp.dot(h, w_ref[:, 3 * width:], preferred_element_type=F32)


def _in_proj(x2, g, w, layer, batch, seq):
    m, d = x2.shape
    t = SEQ_TILE
    tiles = seq // t
    width = N_HEADS * D_V
    q_scale = (D_QK ** -0.5) * LOG2E
    t_shape = (batch, N_HEADS, tiles, D_V, t)
    t_spec = pl.BlockSpec((1, N_HEADS, 1, D_V, t), lambda i: (i // tiles, 0, i % tiles, 0, 0))
    return pl.pallas_call(
        functools.partial(_in_proj_kernel, q_scale=q_scale),
        grid=(m // t,),
        in_specs=[
            pl.BlockSpec((t, d), lambda i: (i, 0)),
            _layer(layer, (1, d), lambda i: (0, 0), resident=True),
            _layer(layer, (d, 4 * width), lambda i: (0, 0), resident=True),
        ],
        out_specs=[
            t_spec,
            pl.BlockSpec((t, width), lambda i: (i, 0)),
            t_spec,
            pl.BlockSpec((t, width), lambda i: (i, 0)),
        ],
        out_shape=[
            jax.ShapeDtypeStruct(t_shape, BF16),
            jax.ShapeDtypeStruct((m, width), BF16),
            jax.ShapeDtypeStruct(t_shape, BF16),
            jax.ShapeDtypeStruct((m, width), F32),
        ],
        compiler_params=_params(1),
        name="in_proj",
    )(x2, g, w)


def _split3(v):
    a = v.astype(BF16).astype(F32)
    b = (v - a).astype(BF16).astype(F32)
    c = (v - a - b).astype(BF16).astype(F32)
    return a, b, c


def _attn_kernel(lq1_ref, lk1_ref, lq2_ref, lk2_ref, gh_ref, qT_ref, k_ref, vT_ref, o_ref,
                 pf_ref, qf_ref, mask_ref, s_ref, acc_ref, stat_ref, *, lam_init):
    t = SEQ_TILE
    w = ATTN_STRIP
    nq = ATTN_Q_TILES
    tq = nq * t
    hd = pl.program_id(1)
    qi = pl.program_id(2)
    base = qi * nq

    def slope_row(n):
        return jnp.exp2(jnp.full((1, n), -8.0 / N_HEADS, F32) * (hd + 1).astype(F32)) * LOG2E

    slope = slope_row(tq)
    q_tile = lax.broadcasted_iota(jnp.int32, (1, tq), 1) // t

    @pl.when(qi == 0)
    def _():
        c = lax.broadcasted_iota(jnp.int32, (t, D_V), 0)
        lane = lax.broadcasted_iota(jnp.int32, (t, D_V), 1)
        c_hi = ((c >> 8) << 8).astype(F32)
        c_lo = (c & 255).astype(F32)
        pf = jnp.where(lane < 3, c_hi, jnp.where(lane < 6, c_lo, jnp.where(lane < 9, 1.0, 0.0)))
        pf_ref[...] = pf.astype(BF16)
        r = lax.broadcasted_iota(jnp.int32, (1, t), 1).astype(F32)
        sp = _split3(slope_row(t))
        gp = _split3(-slope_row(t) * r)
        row = lax.broadcasted_iota(jnp.int32, (D_V, t), 0)
        qf = jnp.zeros((D_V, t), F32)
        for i in range(3):
            qf = jnp.where((row == i) | (row == i + 3), sp[i], qf)
            qf = jnp.where(row == i + 6, gp[i], qf)
        qf_ref[...] = qf.astype(BF16)
        key = lax.broadcasted_iota(jnp.int32, (t, t), 0)
        qry = lax.broadcasted_iota(jnp.int32, (t, t), 1)
        mask_ref[...] = jnp.where(key > qry, NEG_BIG, 0.0)

    row = lax.broadcasted_iota(jnp.int32, (D_V, t), 0)
    qf = qf_ref[...]
    qw = ([], [])
    for b in range(nq):
        qT = qT_ref[0, 0, b]
        zero = jnp.zeros_like(qT)
        qw[0].append(jnp.concatenate([jnp.where(row < D_QK, qT, zero), qf], axis=0))
        qw[1].append(jnp.concatenate([jnp.where(row < D_QK, zero, qT), qf], axis=0))

    acc_ref[...] = jnp.zeros_like(acc_ref)
    stat_ref[0:2] = jnp.full((2, 1, tq), NEG_BIG, F32)

    def strips_from(lo):
        return [(b, h * w) for b in range(lo, nq) for h in range(t // w)]

    def logits(mp, j, stair):
        lo = 0 if stair is None else stair
        start = pl.multiple_of(j * t, t)
        kt = jnp.concatenate([k_ref[pl.ds(start, t), :], pf_ref[...]], axis=1)
        parts = []
        for b, c in strips_from(lo):
            sb = jnp.dot(kt, qw[mp][b][:, c:c + w], preferred_element_type=F32)
            if b == stair:
                sb = sb + mask_ref[:, c:c + w]
            s_ref[mp, :, b * t + c:b * t + c + w] = sb
            parts.append(jnp.max(sb, axis=0, keepdims=True))
        stat_ref[2 + mp, :, lo * t:tq] = jnp.concatenate(parts, axis=1)

    ones = jnp.ones((ATTN_SUM_ROWS, t), BF16)

    def values(mp, j, stair):
        lo = 0 if stair is None else stair
        live = slice(lo * t, tq)
        off = (slope * ((j - base - q_tile) * t).astype(F32))[:, live]
        vt = jnp.concatenate([vT_ref[0, 0, j], ones], axis=0)
        m_old = stat_ref[mp, :, live]
        m_new = jnp.maximum(m_old, stat_ref[2 + mp, :, live] + off)
        stat_ref[mp, :, live] = m_new
        mm = m_new - off
        alpha = jnp.exp2(m_old - m_new)
        for b, c in strips_from(lo):
            cols = slice(b * t + c, b * t + c + w)
            loc = slice((b - lo) * t + c, (b - lo) * t + c + w)
            e = jnp.exp2(s_ref[mp, :, cols] - mm[:, loc]).astype(BF16)
            pv = jnp.dot(vt, e, preferred_element_type=F32)
            acc_ref[mp, :, cols] = acc_ref[mp, :, cols] * alpha[:, loc] + pv

    def step(j, stair, stair_next):
        logits(1, j, stair)
        values(0, j, stair)
        logits(0, j + 1, stair_next)
        values(1, j, stair)

    @pl.when(qi == 0)
    def _():
        logits(0, 0, 0)

    @pl.when(qi > 0)
    def _():
        logits(0, 0, None)

    def body(p, carry):
        step(2 * p + 1, None, None)
        step(2 * p + 2, None, None)
        return carry

    @pl.when(qi > 0)
    def _():
        step(0, None, None)

    lax.fori_loop(0, (base - 2) // 2, body, 0)

    @pl.when(qi > 0)
    def _():
        step(base - 1, None, 0)

    for a in range(nq - 1):
        step(base + a, a, a + 1)
    last = base + nq - 1
    logits(1, last, nq - 1)
    values(0, last, nq - 1)
    values(1, last, nq - 1)

    lam = (jnp.exp(jnp.sum(lq1_ref[...] * lk1_ref[...], axis=-1, keepdims=True))
           - jnp.exp(jnp.sum(lq2_ref[...] * lk2_ref[...], axis=-1, keepdims=True))
           + lam_init)
    for b in range(nq):
        cols = slice(b * t, (b + 1) * t)
        l1 = acc_ref[0, D_V:D_V + 1, cols]
        l2 = acc_ref[1, D_V:D_V + 1, cols]
        o = acc_ref[0, 0:D_V, cols] * (1.0 / l1) - lam * (acc_ref[1, 0:D_V, cols] * (1.0 / l2))
        inv = lax.rsqrt(jnp.mean(o * o, axis=0, keepdims=True) + EPS)
        on = (o * inv) * gh_ref[0] * (1.0 - lam_init)
        o_ref[cols, :] = on.T.astype(BF16)


def _attention(qT, k, vT, gh, lq1, lk1, lq2, lk2, layer, batch, seq, lam_init):
    assert ATTN_Q_TILES % 2 == 0
    t = SEQ_TILE
    nq = ATTN_Q_TILES
    tq = nq * t
    tiles = seq // t
    blocks = seq // tq
    m, width = k.shape
    small = _layer(layer, (1, D_QK), lambda b, h, q: (0, 0), resident=True)
    return pl.pallas_call(
        functools.partial(_attn_kernel, lam_init=lam_init),
        grid=(batch, N_HEADS, blocks),
        in_specs=[
            small, small, small, small,
            _layer(layer, (1, D_V, 1), lambda b, h, q: (h, 0, 0)),
            pl.BlockSpec((1, 1, nq, D_V, t), lambda b, h, q: (b, h, q, 0, 0)),
            pl.BlockSpec((seq, D_V), lambda b, h, q: (b, h)),
            pl.BlockSpec((1, 1, tiles, D_V, t), lambda b, h, q: (b, h, 0, 0, 0)),
        ],
        out_specs=pl.BlockSpec((tq, D_V), lambda b, h, q: (b * blocks + q, h)),
        out_shape=jax.ShapeDtypeStruct((m, width), BF16),
        scratch_shapes=[
            pltpu.VMEM((t, D_V), BF16),
            pltpu.VMEM((D_V, t), BF16),
            pltpu.VMEM((t, t), F32),
            pltpu.VMEM((2, t, tq), F32),
            pltpu.VMEM((2, D_V + ATTN_SUM_ROWS, tq), F32),
            pltpu.VMEM((4, 1, tq), F32),
        ],
        compiler_params=_params(3),
        name="attn",
    )(lq1, lk1, lq2, lk2, gh, qT, k, vT)


def _mix_out_kernel(o_ref, u_ref, uh_ref, x_ref, wp_ref, ps_ref, wo_ref, gpost_ref, gpre_ref,
                    xo_ref, h_ref, *, tiles_per_seq):
    t = SEQ_TILE
    n = MIX_ROW_CHUNK
    st = pl.program_id(0) % tiles_per_seq
    group = u_ref.shape[1] // len(POOL_WINDOWS)
    attn_w = o_ref.shape[1]
    halo = jnp.where(st == 0, 0.0, uh_ref[...])

    for r in range(0, t, n):
        u = u_ref[r:r + n, :]
        prev = halo if r == 0 else u_ref[r - POOL_HALO:r, :]
        ext = jnp.concatenate([prev, u], axis=0)
        pos = (st * t + r + lax.broadcasted_iota(jnp.int32, (n, 1), 0) + 1).astype(F32)
        mix = jnp.dot(o_ref[r:r + n, :], wo_ref[0:attn_w, :], preferred_element_type=F32)
        pooled = []
        for g, w in enumerate(POOL_WINDOWS):
            cols = slice(g * group, (g + 1) * group)
            s = ext[:, cols]
            span = 1
            while span < w:
                s = s + pltpu.roll(s, span, 0)
                span *= 2
            pm = s[POOL_HALO:, :] / jnp.minimum(pos, float(w)) - u[:, cols]
            pw = jnp.dot(pm.astype(BF16), wp_ref[g], preferred_element_type=F32) * ps_ref[:, cols]
            pooled.append(pw.astype(BF16))
        mix = mix + jnp.dot(jnp.concatenate(pooled, axis=1), wo_ref[attn_w:, :], preferred_element_type=F32)
        xn = x_ref[r:r + n, :] + mix * _rms_scale(mix) * gpost_ref[...]
        xo_ref[r:r + n, :] = xn
        h_ref[r:r + n, :] = (xn * _rms_scale(xn) * gpre_ref[...]).astype(BF16)


def _mix_out(o, u, x2, wp, ps, wo, gpost, gpre, layer, seq):
    m, d = x2.shape
    t = SEQ_TILE
    tiles = seq // t
    pw = u.shape[1]
    halo_blocks = t // POOL_HALO
    const = lambda i: (0, 0)
    return pl.pallas_call(
        functools.partial(_mix_out_kernel, tiles_per_seq=tiles),
        grid=(m // t,),
        in_specs=[
            pl.BlockSpec((t, o.shape[1]), lambda i: (i, 0)),
            pl.BlockSpec((t, pw), lambda i: (i, 0)),
            pl.BlockSpec((POOL_HALO, pw), lambda i: (jnp.maximum(i * halo_blocks - 1, 0), 0)),
            pl.BlockSpec((t, d), lambda i: (i, 0)),
            _layer(layer, wp.shape[1:], lambda i: (0, 0, 0), resident=True),
            _layer(layer, (1, pw), const, resident=True),
            _layer(layer, wo.shape[1:], const, resident=True),
            _layer(layer, (1, d), const, resident=True),
            _layer(layer, (1, d), const, resident=True),
        ],
        out_specs=[
            pl.BlockSpec((t, d), lambda i: (i, 0)),
            pl.BlockSpec((t, d), lambda i: (i, 0)),
        ],
        out_shape=[
            jax.ShapeDtypeStruct((m, d), F32),
            jax.ShapeDtypeStruct((m, d), BF16),
        ],
        compiler_params=_params(1),
        name="mix_out",
    )(o, u, u, x2, wp, ps, wo, gpost, gpre)


def _ffn_up_kernel(h_ref, wg_ref, wv_ref, cwg_ref, cwv_ref, cbg_ref, cbv_ref, a_ref,
                   carry_g_ref, carry_v_ref, *, tiles_per_seq):
    st = pl.program_id(1) % tiles_per_seq

    @pl.when(st == 0)
    def _():
        carry_g_ref[...] = jnp.zeros_like(carry_g_ref)
        carry_v_ref[...] = jnp.zeros_like(carry_v_ref)

    def conv(up, prev, cw_ref, cb_ref):
        ext = jnp.concatenate([prev, up], axis=0)
        x1 = pltpu.roll(ext, 1, 0)[CONV_HALO:, :]
        x2 = pltpu.roll(ext, 2, 0)[CONV_HALO:, :]
        return cb_ref[...] + x2 * cw_ref[0:1, :] + x1 * cw_ref[1:2, :] + up * cw_ref[2:3, :]

    prev_g = carry_g_ref[...]
    prev_v = carry_v_ref[...]
    rows = h_ref.shape[0]
    for r in range(0, rows, FFN_ROW_CHUNK):
        h = h_ref[r:r + FFN_ROW_CHUNK, :]
        up_g = jnp.dot(h, wg_ref[...], preferred_element_type=F32)
        up_v = jnp.dot(h, wv_ref[...], preferred_element_type=F32)
        gate = conv(up_g, prev_g, cwg_ref, cbg_ref)
        val = conv(up_v, prev_v, cwv_ref, cbv_ref)
        a_ref[r:r + FFN_ROW_CHUNK, :] = (jax.nn.gelu(gate, approximate=True) * val).astype(BF16)
        prev_g = up_g[FFN_ROW_CHUNK - CONV_HALO:, :]
        prev_v = up_v[FFN_ROW_CHUNK - CONV_HALO:, :]
    carry_g_ref[...] = prev_g
    carry_v_ref[...] = prev_v


def _ffn_up(h, w_up, conv_w, conv_b, layer, seq):
    m, d = h.shape
    d_ff = w_up.shape[2] // 2
    tm, tn = FFN_ROW_TILE, FFN_COL_TILE
    nb = d_ff // tn
    tiles = seq // tm
    return pl.pallas_call(
        functools.partial(_ffn_up_kernel, tiles_per_seq=tiles),
        grid=(nb, m // tm),
        in_specs=[
            pl.BlockSpec((tm, d), lambda n, i: (i, 0)),
            _layer(layer, (d, tn), lambda n, i: (0, n)),
            _layer(layer, (d, tn), lambda n, i: (0, n + nb)),
            _layer(layer, (CONV_WIDTH, tn), lambda n, i: (0, n)),
            _layer(layer, (CONV_WIDTH, tn), lambda n, i: (0, n + nb)),
            _layer(layer, (1, tn), lambda n, i: (0, n)),
            _layer(layer, (1, tn), lambda n, i: (0, n + nb)),
        ],
        out_specs=pl.BlockSpec((tm, tn), lambda n, i: (i, n)),
        out_shape=jax.ShapeDtypeStruct((m, d_ff), BF16),
        scratch_shapes=[
            pltpu.VMEM((CONV_HALO, tn), F32),
            pltpu.VMEM((CONV_HALO, tn), F32),
        ],
        compiler_params=_params(2),
        name="ffn_up",
    )(h, w_up, w_up, conv_w, conv_w, conv_b, conv_b)


def _ffn_down_kernel(a_ref, w_ref, x_ref, g_ref, xo_ref):
    y = jnp.dot(a_ref[...], w_ref[...], preferred_element_type=F32)
    xo_ref[...] = x_ref[...] + y * _rms_scale(y) * g_ref[...]


def _ffn_down(a, w, x2, g, layer):
    m, d = x2.shape
    d_ff = a.shape[1]
    tm = DOWN_ROW_TILE
    return pl.pallas_call(
        _ffn_down_kernel,
        grid=(m // tm,),
        in_specs=[
            pl.BlockSpec((tm, d_ff), lambda i: (i, 0)),
            _layer(layer, (d_ff, d), lambda i: (0, 0), resident=True),
            pl.BlockSpec((tm, d), lambda i: (i, 0)),
            _layer(layer, (1, d), lambda i: (0, 0), resident=True),
        ],
        out_specs=pl.BlockSpec((tm, d), lambda i: (i, 0)),
        out_shape=jax.ShapeDtypeStruct((m, d), F32),
        compiler_params=_params(1),
        name="ffn_down",
    )(a, w, x2, g)


def kernel(x, g_mix_pre, w_in, lam_q1, lam_k1, lam_q2, lam_k2, g_head, w_pool, pool_scale,
           w_out, g_mix_post, g_ffn_pre, w_up, conv_w, conv_b, w_down, g_ffn_post):
    batch, seq, d = x.shape
    depth = w_in.shape[0]
    assert seq % FFN_ROW_TILE == 0 and seq % (SEQ_TILE * ATTN_Q_TILES) == 0
    assert g_head.shape[1] == N_HEADS * D_V
    assert (w_up.shape[2] // 2) % FFN_COL_TILE == 0

    w_in_b = w_in.astype(BF16)
    w_pool_b = w_pool.astype(BF16)
    w_out_b = w_out.astype(BF16)
    w_up_b = w_up.astype(BF16)
    w_down_b = w_down.astype(BF16)
    rows = lambda p: p[:, None, :]
    g_mix_pre, g_mix_post, g_ffn_pre, g_ffn_post = map(rows, (g_mix_pre, g_mix_post, g_ffn_pre, g_ffn_post))
    lam_q1, lam_k1, lam_q2, lam_k2 = map(rows, (lam_q1, lam_k1, lam_q2, lam_k2))
    pool_scale, conv_b = rows(pool_scale), rows(conv_b)
    g_head = g_head.reshape(depth, N_HEADS, D_V, 1)

    x2 = x.reshape(batch * seq, d)
    for i in range(depth):
        lam_init = 0.8 - 0.6 * math.exp(-0.3 * i)
        qT, k, vT, u = _in_proj(x2, g_mix_pre, w_in_b, i, batch, seq)
        o = _attention(qT, k, vT, g_head, lam_q1, lam_k1, lam_q2, lam_k2, i, batch, seq, lam_init)
        x2, h = _mix_out(o, u, x2, w_pool_b, pool_scale, w_out_b, g_mix_post, g_ffn_pre, i, seq)
        a = _ffn_up(h, w_up_b, conv_w, conv_b, i, seq)
        x2 = _ffn_down(a, w_down_b, x2, g_ffn_post, i)
    return x2.reshape(batch, seq, d)
```

```python
import functools
import math

import jax
import jax.numpy as jnp
from jax import lax
from jax.experimental import pallas as pl
from jax.experimental.pallas import tpu as pltpu

N_HEADS = 8
D_V = 128
D_QK = 64
POOL_WINDOWS = (2, 4, 8, 16)
POOL_HALO = 16
CONV_WIDTH = 3
CONV_HALO = 8
EPS = 1e-6
LOG2E = math.log2(math.e)
NEG_BIG = -1e30

SEQ_TILE = 512
ATTN_STRIP = 256
ATTN_SUM_ROWS = 16
ATTN_Q_TILES = 4
MIX_ROW_CHUNK = 256
FFN_ROW_TILE = 2048
FFN_ROW_CHUNKS = (1024, 1024)
FFN_COL_TILE = 512
DOWN_ROW_TILE = 256
VMEM_LIMIT_BYTES = 56 * 1024 * 1024

F32 = jnp.float32
BF16 = jnp.bfloat16


def _params(n_axes):
    return pltpu.CompilerParams(
        dimension_semantics=("arbitrary",) * n_axes,
        vmem_limit_bytes=VMEM_LIMIT_BYTES,
    )


def _layer(layer, block_shape, index_map, resident=False):
    return pl.BlockSpec((None,) + tuple(block_shape), lambda *ids: (layer,) + tuple(index_map(*ids)),
                        pipeline_mode=pl.Buffered(1) if resident else None)


def _rms_scale(x):
    return lax.rsqrt(jnp.mean(x * x, axis=-1, keepdims=True) + EPS)


def _in_proj_kernel(x_ref, g_ref, w_ref, qT_ref, k_ref, vT_ref, u_ref, *, q_scale):
    x = x_ref[...]
    h = (x * _rms_scale(x) * g_ref[...]).astype(BF16)
    width = N_HEADS * D_V
    yq = jnp.dot(h, w_ref[:, 0:width], preferred_element_type=F32) * q_scale
    for hd in range(N_HEADS):
        qT_ref[0, hd, 0] = yq[:, hd * D_V:(hd + 1) * D_V].T.astype(BF16)
    yk = jnp.dot(h, w_ref[:, width:2 * width], preferred_element_type=F32)
    k_ref[...] = yk.astype(BF16)
    yv = jnp.dot(h, w_ref[:, 2 * width:3 * width], preferred_element_type=F32)
    for hd in range(N_HEADS):
        vT_ref[0, hd, 0] = yv[:, hd * D_V:(hd + 1) * D_V].T.astype(BF16)
    u_ref[...] = jnp.dot(h, w_ref[:, 3 * width:], preferred_element_type=F32)


def _in_proj(x2, g, w, layer, batch, seq):
    m, d = x2.shape
    t = SEQ_TILE
    tiles = seq // t
    width = N_HEADS * D_V
    q_scale = (D_QK ** -0.5) * LOG2E
    t_shape = (batch, N_HEADS, tiles, D_V, t)
    t_spec = pl.BlockSpec((1, N_HEADS, 1, D_V, t), lambda i: (i // tiles, 0, i % tiles, 0, 0))
    return pl.pallas_call(
        functools.partial(_in_proj_kernel, q_scale=q_scale),
        grid=(m // t,),
        in_specs=[
            pl.BlockSpec((t, d), lambda i: (i, 0)),
            _layer(layer, (1, d), lambda i: (0, 0), resident=True),
            _layer(layer, (d, 4 * width), lambda i: (0, 0), resident=True),
        ],
        out_specs=[
            t_spec,
            pl.BlockSpec((t, width), lambda i: (i, 0)),
            t_spec,
            pl.BlockSpec((t, width), lambda i: (i, 0)),
        ],
        out_shape=[
            jax.ShapeDtypeStruct(t_shape, BF16),
            jax.ShapeDtypeStruct((m, width), BF16),
            jax.ShapeDtypeStruct(t_shape, BF16),
            jax.ShapeDtypeStruct((m, width), F32),
        ],
        compiler_params=_params(1),
        name="in_proj",
    )(x2, g, w)


def _split3(v):
    a = v.astype(BF16).astype(F32)
    b = (v - a).astype(BF16).astype(F32)
    c = (v - a - b).astype(BF16).astype(F32)
    return a, b, c


def _attn_kernel(lq1_ref, lk1_ref, lq2_ref, lk2_ref, gh_ref, qT_ref, k_ref, vT_ref, o_ref,
                 pf_ref, qf_ref, mask_ref, s_ref, acc_ref, stat_ref, *, lam_init):
    t = SEQ_TILE
    w = ATTN_STRIP
    nq = ATTN_Q_TILES
    tq = nq * t
    hd = pl.program_id(1)
    qi = pl.program_id(2)
    base = qi * nq

    def slope_row(n):
        return jnp.exp2(jnp.full((1, n), -8.0 / N_HEADS, F32) * (hd + 1).astype(F32)) * LOG2E

    slope = slope_row(tq)
    q_tile = lax.broadcasted_iota(jnp.int32, (1, tq), 1) // t

    @pl.when(qi == 0)
    def _():
        c = lax.broadcasted_iota(jnp.int32, (t, D_V), 0)
        lane = lax.broadcasted_iota(jnp.int32, (t, D_V), 1)
        c_hi = ((c >> 8) << 8).astype(F32)
        c_lo = (c & 255).astype(F32)
        pf = jnp.where(lane < 3, c_hi, jnp.where(lane < 6, c_lo, jnp.where(lane < 9, 1.0, 0.0)))
        pf_ref[...] = pf.astype(BF16)
        r = lax.broadcasted_iota(jnp.int32, (1, t), 1).astype(F32)
        sp = _split3(slope_row(t))
        gp = _split3(-slope_row(t) * r)
        row = lax.broadcasted_iota(jnp.int32, (D_V, t), 0)
        qf = jnp.zeros((D_V, t), F32)
        for i in range(3):
            qf = jnp.where((row == i) | (row == i + 3), sp[i], qf)
            qf = jnp.where(row == i + 6, gp[i], qf)
        qf_ref[...] = qf.astype(BF16)
        key = lax.broadcasted_iota(jnp.int32, (t, t), 0)
        qry = lax.broadcasted_iota(jnp.int32, (t, t), 1)
        mask_ref[...] = jnp.where(key > qry, NEG_BIG, 0.0)

    row = lax.broadcasted_iota(jnp.int32, (D_V, t), 0)
    qf = qf_ref[...]
    qw = ([], [])
    for b in range(nq):
        qT = qT_ref[0, 0, b]
        zero = jnp.zeros_like(qT)
        qw[0].append(jnp.concatenate([jnp.where(row < D_QK, qT, zero), qf], axis=0))
        qw[1].append(jnp.concatenate([jnp.where(row < D_QK, zero, qT), qf], axis=0))

    acc_ref[...] = jnp.zeros_like(acc_ref)
    stat_ref[0:2] = jnp.full((2, 1, tq), NEG_BIG, F32)

    def strips_from(lo):
        return [(b, h * w) for b in range(lo, nq) for h in range(t // w)]

    def visible_keys(b, c, stair):
        return min(t, c + w) if b == stair else t

    def logits(mp, j, stair):
        lo = 0 if stair is None else stair
        start = pl.multiple_of(j * t, t)
        kt = jnp.concatenate([k_ref[pl.ds(start, t), :], pf_ref[...]], axis=1)
        parts = []
        for b, c in strips_from(lo):
            keys = visible_keys(b, c, stair)
            sb = jnp.dot(kt[0:keys, :], qw[mp][b][:, c:c + w], preferred_element_type=F32)
            if b == stair:
                sb = sb + mask_ref[0:keys, c:c + w]
            s_ref[mp, 0:keys, b * t + c:b * t + c + w] = sb
            parts.append(jnp.max(sb, axis=0, keepdims=True))
        stat_ref[2 + mp, :, lo * t:tq] = jnp.concatenate(parts, axis=1)

    ones = jnp.ones((ATTN_SUM_ROWS, t), BF16)

    def values(mp, j, stair):
        lo = 0 if stair is None else stair
        live = slice(lo * t, tq)
        off = (slope * ((j - base - q_tile) * t).astype(F32))[:, live]
        vt = jnp.concatenate([vT_ref[0, 0, j], ones], axis=0)
        m_old = stat_ref[mp, :, live]
        m_new = jnp.maximum(m_old, stat_ref[2 + mp, :, live] + off)
        stat_ref[mp, :, live] = m_new
        mm = m_new - off
        alpha = jnp.exp2(m_old - m_new)
        for b, c in strips_from(lo):
            cols = slice(b * t + c, b * t + c + w)
            loc = slice((b - lo) * t + c, (b - lo) * t + c + w)
            keys = visible_keys(b, c, stair)
            e = jnp.exp2(s_ref[mp, 0:keys, cols] - mm[:, loc]).astype(BF16)
            pv = jnp.dot(vt[:, 0:keys], e, preferred_element_type=F32)
            acc_ref[mp, :, cols] = acc_ref[mp, :, cols] * alpha[:, loc] + pv

    def step(j, stair, stair_next):
        logits(1, j, stair)
        values(0, j, stair)
        logits(0, j + 1, stair_next)
        values(1, j, stair)

    @pl.when(qi == 0)
    def _():
        logits(0, 0, 0)

    @pl.when(qi > 0)
    def _():
        logits(0, 0, None)

    def body(p, carry):
        step(2 * p + 1, None, None)
        step(2 * p + 2, None, None)
        return carry

    @pl.when(qi > 0)
    def _():
        step(0, None, None)

    lax.fori_loop(0, (base - 2) // 2, body, 0)

    @pl.when(qi > 0)
    def _():
        step(base - 1, None, 0)

    for a in range(nq - 1):
        step(base + a, a, a + 1)
    last = base + nq - 1
    logits(1, last, nq - 1)
    values(0, last, nq - 1)
    values(1, last, nq - 1)

    lam = (jnp.exp(jnp.sum(lq1_ref[...] * lk1_ref[...], axis=-1, keepdims=True))
           - jnp.exp(jnp.sum(lq2_ref[...] * lk2_ref[...], axis=-1, keepdims=True))
           + lam_init)
    for b in range(nq):
        cols = slice(b * t, (b + 1) * t)
        l1 = acc_ref[0, D_V:D_V + 1, cols]
        l2 = acc_ref[1, D_V:D_V + 1, cols]
        o = acc_ref[0, 0:D_V, cols] * (1.0 / l1) - lam * (acc_ref[1, 0:D_V, cols] * (1.0 / l2))
        inv = lax.rsqrt(jnp.mean(o * o, axis=0, keepdims=True) + EPS)
        on = (o * inv) * gh_ref[0] * (1.0 - lam_init)
        o_ref[cols, :] = on.T.astype(BF16)


def _attention(qT, k, vT, gh, lq1, lk1, lq2, lk2, layer, batch, seq, lam_init):
    assert ATTN_Q_TILES % 2 == 0
    t = SEQ_TILE
    nq = ATTN_Q_TILES
    tq = nq * t
    tiles = seq // t
    blocks = seq // tq
    m, width = k.shape
    small = _layer(layer, (1, D_QK), lambda b, h, q: (0, 0), resident=True)
    return pl.pallas_call(
        functools.partial(_attn_kernel, lam_init=lam_init),
        grid=(batch, N_HEADS, blocks),
        in_specs=[
            small, small, small, small,
            _layer(layer, (1, D_V, 1), lambda b, h, q: (h, 0, 0)),
            pl.BlockSpec((1, 1, nq, D_V, t), lambda b, h, q: (b, h, q, 0, 0)),
            pl.BlockSpec((seq, D_V), lambda b, h, q: (b, h)),
            pl.BlockSpec((1, 1, tiles, D_V, t), lambda b, h, q: (b, h, 0, 0, 0)),
        ],
        out_specs=pl.BlockSpec((tq, D_V), lambda b, h, q: (b * blocks + q, h)),
        out_shape=jax.ShapeDtypeStruct((m, width), BF16),
        scratch_shapes=[
            pltpu.VMEM((t, D_V), BF16),
            pltpu.VMEM((D_V, t), BF16),
            pltpu.VMEM((t, t), F32),
            pltpu.VMEM((2, t, tq), F32),
            pltpu.VMEM((2, D_V + ATTN_SUM_ROWS, tq), F32),
            pltpu.VMEM((4, 1, tq), F32),
        ],
        compiler_params=_params(3),
        name="attn",
    )(lq1, lk1, lq2, lk2, gh, qT, k, vT)


def _mix_out_kernel(o_ref, u_ref, uh_ref, x_ref, wp_ref, ps_ref, wo_ref, gpost_ref, gpre_ref,
                    xo_ref, h_ref, *, tiles_per_seq):
    t = SEQ_TILE
    n = MIX_ROW_CHUNK
    st = pl.program_id(0) % tiles_per_seq
    group = u_ref.shape[1] // len(POOL_WINDOWS)
    attn_w = o_ref.shape[1]
    halo = jnp.where(st == 0, 0.0, uh_ref[...])

    for r in range(0, t, n):
        u = u_ref[r:r + n, :]
        prev = halo if r == 0 else u_ref[r - POOL_HALO:r, :]
        ext = jnp.concatenate([prev, u], axis=0)
        pos = (st * t + r + lax.broadcasted_iota(jnp.int32, (n, 1), 0) + 1).astype(F32)
        mix = jnp.dot(o_ref[r:r + n, :], wo_ref[0:attn_w, :], preferred_element_type=F32)
        pooled = []
        for g, w in enumerate(POOL_WINDOWS):
            cols = slice(g * group, (g + 1) * group)
            s = ext[:, cols]
            span = 1
            while span < w:
                s = s + pltpu.roll(s, span, 0)
                span *= 2
            pm = s[POOL_HALO:, :] / jnp.minimum(pos, float(w)) - u[:, cols]
            pw = jnp.dot(pm.astype(BF16), wp_ref[g], preferred_element_type=F32) * ps_ref[:, cols]
            pooled.append(pw.astype(BF16))
        mix = mix + jnp.dot(jnp.concatenate(pooled, axis=1), wo_ref[attn_w:, :], preferred_element_type=F32)
        xn = x_ref[r:r + n, :] + mix * _rms_scale(mix) * gpost_ref[...]
        xo_ref[r:r + n, :] = xn
        h_ref[r:r + n, :] = (xn * _rms_scale(xn) * gpre_ref[...]).astype(BF16)


def _mix_out(o, u, x2, wp, ps, wo, gpost, gpre, layer, seq):
    m, d = x2.shape
    t = SEQ_TILE
    tiles = seq // t
    pw = u.shape[1]
    halo_blocks = t // POOL_HALO
    const = lambda i: (0, 0)
    return pl.pallas_call(
        functools.partial(_mix_out_kernel, tiles_per_seq=tiles),
        grid=(m // t,),
        in_specs=[
            pl.BlockSpec((t, o.shape[1]), lambda i: (i, 0)),
            pl.BlockSpec((t, pw), lambda i: (i, 0)),
            pl.BlockSpec((POOL_HALO, pw), lambda i: (jnp.maximum(i * halo_blocks - 1, 0), 0)),
            pl.BlockSpec((t, d), lambda i: (i, 0)),
            _layer(layer, wp.shape[1:], lambda i: (0, 0, 0), resident=True),
            _layer(layer, (1, pw), const, resident=True),
            _layer(layer, wo.shape[1:], const, resident=True),
            _layer(layer, (1, d), const, resident=True),
            _layer(layer, (1, d), const, resident=True),
        ],
        out_specs=[
            pl.BlockSpec((t, d), lambda i: (i, 0)),
            pl.BlockSpec((t, d), lambda i: (i, 0)),
        ],
        out_shape=[
            jax.ShapeDtypeStruct((m, d), F32),
            jax.ShapeDtypeStruct((m, d), BF16),
        ],
        compiler_params=_params(1),
        name="mix_out",
    )(o, u, u, x2, wp, ps, wo, gpost, gpre)


def _ffn_up_kernel(h_ref, wg_ref, wv_ref, cwg_ref, cwv_ref, cbg_ref, cbv_ref, a_ref,
                   carry_g_ref, carry_v_ref, *, tiles_per_seq):
    st = pl.program_id(1) % tiles_per_seq

    @pl.when(st == 0)
    def _():
        carry_g_ref[...] = jnp.zeros_like(carry_g_ref)
        carry_v_ref[...] = jnp.zeros_like(carry_v_ref)

    def conv(up, prev, cw_ref, cb_ref):
        ext = jnp.concatenate([prev, up], axis=0)
        x1 = pltpu.roll(ext, 1, 0)[CONV_HALO:, :]
        x2 = pltpu.roll(ext, 2, 0)[CONV_HALO:, :]
        return cb_ref[...] + x2 * cw_ref[0:1, :] + x1 * cw_ref[1:2, :] + up * cw_ref[2:3, :]

    prev_g = carry_g_ref[...]
    prev_v = carry_v_ref[...]
    assert sum(FFN_ROW_CHUNKS) == h_ref.shape[0]
    r = 0
    for n in FFN_ROW_CHUNKS:
        h = h_ref[r:r + n, :]
        up_g = jnp.dot(h, wg_ref[...], preferred_element_type=F32)
        up_v = jnp.dot(h, wv_ref[...], preferred_element_type=F32)
        gate = conv(up_g, prev_g, cwg_ref, cbg_ref)
        val = conv(up_v, prev_v, cwv_ref, cbv_ref)
        a_ref[r:r + n, :] = (jax.nn.gelu(gate, approximate=True) * val).astype(BF16)
        prev_g = up_g[n - CONV_HALO:, :]
        prev_v = up_v[n - CONV_HALO:, :]
        r += n
    carry_g_ref[...] = prev_g
    carry_v_ref[...] = prev_v


def _ffn_up(h, w_up, conv_w, conv_b, layer, seq):
    m, d = h.shape
    d_ff = w_up.shape[2] // 2
    tm, tn = FFN_ROW_TILE, FFN_COL_TILE
    nb = d_ff // tn
    tiles = seq // tm
    return pl.pallas_call(
        functools.partial(_ffn_up_kernel, tiles_per_seq=tiles),
        grid=(nb, m // tm),
        in_specs=[
            pl.BlockSpec((tm, d), lambda n, i: (i, 0)),
            _layer(layer, (d, tn), lambda n, i: (0, n)),
            _layer(layer, (d, tn), lambda n, i: (0, n + nb)),
            _layer(layer, (CONV_WIDTH, tn), lambda n, i: (0, n)),
            _layer(layer, (CONV_WIDTH, tn), lambda n, i: (0, n + nb)),
            _layer(layer, (1, tn), lambda n, i: (0, n)),
            _layer(layer, (1, tn), lambda n, i: (0, n + nb)),
        ],
        out_specs=pl.BlockSpec((tm, tn), lambda n, i: (i, n)),
        out_shape=jax.ShapeDtypeStruct((m, d_ff), BF16),
        scratch_shapes=[
            pltpu.VMEM((CONV_HALO, tn), F32),
            pltpu.VMEM((CONV_HALO, tn), F32),
        ],
        compiler_params=_params(2),
        name="ffn_up",
    )(h, w_up, w_up, conv_w, conv_w, conv_b, conv_b)


def _ffn_down_kernel(a_ref, w_ref, x_ref, g_ref, xo_ref):
    y = jnp.dot(a_ref[...], w_ref[...], preferred_element_type=F32)
    xo_ref[...] = x_ref[...] + y * _rms_scale(y) * g_ref[...]


def _ffn_down(a, w, x2, g, layer):
    m, d = x2.shape
    d_ff = a.shape[1]
    tm = DOWN_ROW_TILE
    return pl.pallas_call(
        _ffn_down_kernel,
        grid=(m // tm,),
        in_specs=[
            pl.BlockSpec((tm, d_ff), lambda i: (i, 0)),
            _layer(layer, (d_ff, d), lambda i: (0, 0), resident=True),
            pl.BlockSpec((tm, d), lambda i: (i, 0)),
            _layer(layer, (1, d), lambda i: (0, 0), resident=True),
        ],
        out_specs=pl.BlockSpec((tm, d), lambda i: (i, 0)),
        out_shape=jax.ShapeDtypeStruct((m, d), F32),
        compiler_params=_params(1),
        name="ffn_down",
    )(a, w, x2, g)


def kernel(x, g_mix_pre, w_in, lam_q1, lam_k1, lam_q2, lam_k2, g_head, w_pool, pool_scale,
           w_out, g_mix_post, g_ffn_pre, w_up, conv_w, conv_b, w_down, g_ffn_post):
    batch, seq, d = x.shape
    depth = w_in.shape[0]
    assert seq % FFN_ROW_TILE == 0 and seq % (SEQ_TILE * ATTN_Q_TILES) == 0
    assert g_head.shape[1] == N_HEADS * D_V
    assert (w_up.shape[2] // 2) % FFN_COL_TILE == 0

    w_in_b = w_in.astype(BF16)
    w_pool_b = w_pool.astype(BF16)
    w_out_b = w_out.astype(BF16)
    w_up_b = w_up.astype(BF16)
    w_down_b = w_down.astype(BF16)
    rows = lambda p: p[:, None, :]
    g_mix_pre, g_mix_post, g_ffn_pre, g_ffn_post = map(rows, (g_mix_pre, g_mix_post, g_ffn_pre, g_ffn_post))
    lam_q1, lam_k1, lam_q2, lam_k2 = map(rows, (lam_q1, lam_k1, lam_q2, lam_k2))
    pool_scale, conv_b = rows(pool_scale), rows(conv_b)
    g_head = g_head.reshape(depth, N_HEADS, D_V, 1)

    x2 = x.reshape(batch * seq, d)
    for i in range(depth):
        lam_init = 0.8 - 0.6 * math.exp(-0.3 * i)
        qT, k, vT, u = _in_proj(x2, g_mix_pre, w_in_b, i, batch, seq)
        o = _attention(qT, k, vT, g_head, lam_q1, lam_k1, lam_q2, lam_k2, i, batch, seq, lam_init)
        x2, h = _mix_out(o, u, x2, w_pool_b, pool_scale, w_out_b, g_mix_post, g_ffn_pre, i, seq)
        a = _ffn_up(h, w_up_b, conv_w, conv_b, i, seq)
        x2 = _ffn_down(a, w_down_b, x2, g_ffn_post, i)
    return x2.reshape(batch, seq, d)
```

```python
import functools
import math

import jax
import jax.numpy as jnp
from jax import lax
from jax.experimental import pallas as pl
from jax.experimental.pallas import tpu as pltpu

N_HEADS = 8
D_V = 128
D_QK = 64
POOL_WINDOWS = (2, 4, 8, 16)
POOL_HALO = 16
CONV_WIDTH = 3
CONV_HALO = 8
EPS = 1e-6
LOG2E = math.log2(math.e)
NEG_BIG = -1e30

SEQ_TILE = 512
ATTN_STRIP = 256
ATTN_SUM_ROWS = 16
ATTN_Q_TILES = 4
MIX_ROW_CHUNK = 256
FFN_ROW_TILE = 2048
FFN_ROW_CHUNKS = (1024, 1024)
FFN_COL_TILE = 512
DOWN_ROW_TILE = 256
VMEM_LIMIT_BYTES = 56 * 1024 * 1024

F32 = jnp.float32
BF16 = jnp.bfloat16


def _params(n_axes):
    return pltpu.CompilerParams(
        dimension_semantics=("arbitrary",) * n_axes,
        vmem_limit_bytes=VMEM_LIMIT_BYTES,
    )


def _layer(layer, block_shape, index_map, resident=False):
    return pl.BlockSpec((None,) + tuple(block_shape), lambda *ids: (layer,) + tuple(index_map(*ids)),
                        pipeline_mode=pl.Buffered(1) if resident else None)


def _rms_scale(x):
    return lax.rsqrt(jnp.mean(x * x, axis=-1, keepdims=True) + EPS)


def _in_proj_kernel(x_ref, g_ref, w_ref, qT_ref, k_ref, vT_ref, u_ref, *, q_scale):
    x = x_ref[...]
    h = (x * _rms_scale(x) * g_ref[...]).astype(BF16)
    width = N_HEADS * D_V
    yq = jnp.dot(h, w_ref[:, 0:width], preferred_element_type=F32) * q_scale
    for hd in range(N_HEADS):
        qT_ref[0, hd, 0] = yq[:, hd * D_V:(hd + 1) * D_V].T.astype(BF16)
    yk = jnp.dot(h, w_ref[:, width:2 * width], preferred_element_type=F32)
    k_ref[...] = yk.astype(BF16)
    yv = jnp.dot(h, w_ref[:, 2 * width:3 * width], preferred_element_type=F32)
    for hd in range(N_HEADS):
        vT_ref[0, hd, 0] = yv[:, hd * D_V:(hd + 1) * D_V].T.astype(BF16)
    u_ref[...] = jnp.dot(h, w_ref[:, 3 * width:], preferred_element_type=F32)


def _in_proj(x2, g, w, layer, batch, seq):
    m, d = x2.shape
    t = SEQ_TILE
    tiles = seq // t
    width = N_HEADS * D_V
    q_scale = (D_QK ** -0.5) * LOG2E
    t_shape = (batch, N_HEADS, tiles, D_V, t)
    t_spec = pl.BlockSpec((1, N_HEADS, 1, D_V, t), lambda i: (i // tiles, 0, i % tiles, 0, 0))
    return pl.pallas_call(
        functools.partial(_in_proj_kernel, q_scale=q_scale),
        grid=(m // t,),
        in_specs=[
            pl.BlockSpec((t, d), lambda i: (i, 0)),
            _layer(layer, (1, d), lambda i: (0, 0), resident=True),
            _layer(layer, (d, 4 * width), lambda i: (0, 0), resident=True),
        ],
        out_specs=[
            t_spec,
            pl.BlockSpec((t, width), lambda i: (i, 0)),
            t_spec,
            pl.BlockSpec((t, width), lambda i: (i, 0)),
        ],
        out_shape=[
            jax.ShapeDtypeStruct(t_shape, BF16),
            jax.ShapeDtypeStruct((m, width), BF16),
            jax.ShapeDtypeStruct(t_shape, BF16),
            jax.ShapeDtypeStruct((m, width), F32),
        ],
        compiler_params=_params(1),
        name="in_proj",
    )(x2, g, w)


def _split3(v):
    a = v.astype(BF16).astype(F32)
    b = (v - a).astype(BF16).astype(F32)
    c = (v - a - b).astype(BF16).astype(F32)
    return a, b, c


def _attn_kernel(lq1_ref, lk1_ref, lq2_ref, lk2_ref, gh_ref, qT_ref, k_ref, vT_ref, o_ref,
                 pf_ref, qf_ref, mask_ref, s_ref, acc_ref, stat_ref, *, lam_init):
    t = SEQ_TILE
    w = ATTN_STRIP
    nq = ATTN_Q_TILES
    tq = nq * t
    hd = pl.program_id(1)
    qi = pl.program_id(2)
    base = qi * nq

    def slope_row(n):
        return jnp.exp2(jnp.full((1, n), -8.0 / N_HEADS, F32) * (hd + 1).astype(F32)) * LOG2E

    slope = slope_row(tq)
    q_tile = lax.broadcasted_iota(jnp.int32, (1, tq), 1) // t

    @pl.when(qi == 0)
    def _():
        c = lax.broadcasted_iota(jnp.int32, (t, D_V), 0)
        lane = lax.broadcasted_iota(jnp.int32, (t, D_V), 1)
        c_hi = ((c >> 8) << 8).astype(F32)
        c_lo = (c & 255).astype(F32)
        pf = jnp.where(lane < 3, c_hi, jnp.where(lane < 6, c_lo, jnp.where(lane < 9, 1.0, 0.0)))
        pf_ref[...] = pf.astype(BF16)
        r = lax.broadcasted_iota(jnp.int32, (1, t), 1).astype(F32)
        sp = _split3(slope_row(t))
        gp = _split3(-slope_row(t) * r)
        row = lax.broadcasted_iota(jnp.int32, (D_V, t), 0)
        qf = jnp.zeros((D_V, t), F32)
        for i in range(3):
            qf = jnp.where((row == i) | (row == i + 3), sp[i], qf)
            qf = jnp.where(row == i + 6, gp[i], qf)
        qf_ref[...] = qf.astype(BF16)
        key = lax.broadcasted_iota(jnp.int32, (t, t), 0)
        qry = lax.broadcasted_iota(jnp.int32, (t, t), 1)
        mask_ref[...] = jnp.where(key > qry, NEG_BIG, 0.0)

    row = lax.broadcasted_iota(jnp.int32, (D_V, t), 0)
    qf = qf_ref[...]
    qw = ([], [])
    for b in range(nq):
        qT = qT_ref[0, 0, b]
        zero = jnp.zeros_like(qT)
        qw[0].append(jnp.concatenate([jnp.where(row < D_QK, qT, zero), qf], axis=0))
        qw[1].append(jnp.concatenate([jnp.where(row < D_QK, zero, qT), qf], axis=0))

    acc_ref[...] = jnp.zeros_like(acc_ref)
    stat_ref[0:2] = jnp.full((2, 1, tq), NEG_BIG, F32)

    def strips_from(lo):
        return [(b, h * w) for b in range(lo, nq) for h in range(t // w)]

    def visible_keys(b, c, stair):
        return min(t, c + w) if b == stair else t

    def logits(mp, j, stair):
        lo = 0 if stair is None else stair
        start = pl.multiple_of(j * t, t)
        kt = jnp.concatenate([k_ref[pl.ds(start, t), :], pf_ref[...]], axis=1)
        parts = []
        for b, c in strips_from(lo):
            keys = visible_keys(b, c, stair)
            sb = jnp.dot(kt[0:keys, :], qw[mp][b][:, c:c + w], preferred_element_type=F32)
            if b == stair:
                sb = sb + mask_ref[0:keys, c:c + w]
            s_ref[mp, 0:keys, b * t + c:b * t + c + w] = sb
            parts.append(jnp.max(sb, axis=0, keepdims=True))
        stat_ref[2 + mp, :, lo * t:tq] = jnp.concatenate(parts, axis=1)

    ones = jnp.ones((ATTN_SUM_ROWS, t), BF16)

    def values(mp, j, stair):
        lo = 0 if stair is None else stair
        live = slice(lo * t, tq)
        off = (slope * ((j - base - q_tile) * t).astype(F32))[:, live]
        vt = jnp.concatenate([vT_ref[0, 0, j], ones], axis=0)
        m_old = stat_ref[mp, :, live]
        m_new = jnp.maximum(m_old, stat_ref[2 + mp, :, live] + off)
        stat_ref[mp, :, live] = m_new
        mm = m_new - off
        alpha = jnp.exp2(m_old - m_new)
        for b, c in strips_from(lo):
            cols = slice(b * t + c, b * t + c + w)
            loc = slice((b - lo) * t + c, (b - lo) * t + c + w)
            keys = visible_keys(b, c, stair)
            e = jnp.exp2(s_ref[mp, 0:keys, cols] - mm[:, loc]).astype(BF16)
            pv = jnp.dot(vt[:, 0:keys], e, preferred_element_type=F32)
            acc_ref[mp, :, cols] = acc_ref[mp, :, cols] * alpha[:, loc] + pv

    def step(j, stair, stair_next):
        logits(1, j, stair)
        values(0, j, stair)
        logits(0, j + 1, stair_next)
        values(1, j, stair)

    @pl.when(qi == 0)
    def _():
        logits(0, 0, 0)

    @pl.when(qi > 0)
    def _():
        logits(0, 0, None)

    def body(p, carry):
        step(2 * p + 1, None, None)
        step(2 * p + 2, None, None)
        return carry

    @pl.when(qi > 0)
    def _():
        step(0, None, None)

    lax.fori_loop(0, (base - 2) // 2, body, 0)

    @pl.when(qi > 0)
    def _():
        step(base - 1, None, 0)

    for a in range(nq - 1):
        step(base + a, a, a + 1)
    last = base + nq - 1
    logits(1, last, nq - 1)
    values(0, last, nq - 1)
    values(1, last, nq - 1)

    lam = (jnp.exp(jnp.sum(lq1_ref[...] * lk1_ref[...], axis=-1, keepdims=True))
           - jnp.exp(jnp.sum(lq2_ref[...] * lk2_ref[...], axis=-1, keepdims=True))
           + lam_init)
    for b in range(nq):
        cols = slice(b * t, (b + 1) * t)
        l1 = acc_ref[0, D_V:D_V + 1, cols]
        l2 = acc_ref[1, D_V:D_V + 1, cols]
        o = acc_ref[0, 0:D_V, cols] * (1.0 / l1) - lam * (acc_ref[1, 0:D_V, cols] * (1.0 / l2))
        inv = lax.rsqrt(jnp.mean(o * o, axis=0, keepdims=True) + EPS)
        on = (o * inv) * gh_ref[0] * (1.0 - lam_init)
        o_ref[cols, :] = on.T.astype(BF16)


def _attention(qT, k, vT, gh, lq1, lk1, lq2, lk2, layer, batch, seq, lam_init):
    assert ATTN_Q_TILES % 2 == 0
    t = SEQ_TILE
    nq = ATTN_Q_TILES
    tq = nq * t
    tiles = seq // t
    blocks = seq // tq
    m, width = k.shape
    small = _layer(layer, (1, D_QK), lambda b, h, q: (0, 0), resident=True)
    return pl.pallas_call(
        functools.partial(_attn_kernel, lam_init=lam_init),
        grid=(batch, N_HEADS, blocks),
        in_specs=[
            small, small, small, small,
            _layer(layer, (1, D_V, 1), lambda b, h, q: (h, 0, 0)),
            pl.BlockSpec((1, 1, nq, D_V, t), lambda b, h, q: (b, h, q, 0, 0)),
            pl.BlockSpec((seq, D_V), lambda b, h, q: (b, h)),
            pl.BlockSpec((1, 1, tiles, D_V, t), lambda b, h, q: (b, h, 0, 0, 0)),
        ],
        out_specs=pl.BlockSpec((tq, D_V), lambda b, h, q: (b * blocks + q, h)),
        out_shape=jax.ShapeDtypeStruct((m, width), BF16),
        scratch_shapes=[
            pltpu.VMEM((t, D_V), BF16),
            pltpu.VMEM((D_V, t), BF16),
            pltpu.VMEM((t, t), F32),
            pltpu.VMEM((2, t, tq), F32),
            pltpu.VMEM((2, D_V + ATTN_SUM_ROWS, tq), F32),
            pltpu.VMEM((4, 1, tq), F32),
        ],
        compiler_params=_params(3),
        name="attn",
    )(lq1, lk1, lq2, lk2, gh, qT, k, vT)


def _mix_out_kernel(o_ref, u_ref, uh_ref, x_ref, wp_ref, ps_ref, wo_ref, gpost_ref, gpre_ref,
                    xo_ref, h_ref, *, tiles_per_seq):
    t = SEQ_TILE
    n = MIX_ROW_CHUNK
    st = pl.program_id(0) % tiles_per_seq
    group = u_ref.shape[1] // len(POOL_WINDOWS)
    attn_w = o_ref.shape[1]
    halo = jnp.where(st == 0, 0.0, uh_ref[...])

    for r in range(0, t, n):
        u = u_ref[r:r + n, :]
        prev = halo if r == 0 else u_ref[r - POOL_HALO:r, :]
        ext = jnp.concatenate([prev, u], axis=0)
        pos = (st * t + r + lax.broadcasted_iota(jnp.int32, (n, 1), 0) + 1).astype(F32)
        mix = jnp.dot(o_ref[r:r + n, :], wo_ref[0:attn_w, :], preferred_element_type=F32)
        pooled = []
        for g, w in enumerate(POOL_WINDOWS):
            cols = slice(g * group, (g + 1) * group)
            s = ext[:, cols]
            span = 1
            while span < w:
                s = s + pltpu.roll(s, span, 0)
                span *= 2
            pm = s[POOL_HALO:, :] / jnp.minimum(pos, float(w)) - u[:, cols]
            pw = jnp.dot(pm.astype(BF16), wp_ref[g], preferred_element_type=F32) * ps_ref[:, cols]
            pooled.append(pw.astype(BF16))
        mix = mix + jnp.dot(jnp.concatenate(pooled, axis=1), wo_ref[attn_w:, :], preferred_element_type=F32)
        xn = x_ref[r:r + n, :] + mix * _rms_scale(mix) * gpost_ref[...]
        xo_ref[r:r + n, :] = xn
        h_ref[r:r + n, :] = (xn * _rms_scale(xn) * gpre_ref[...]).astype(BF16)


def _mix_out(o, u, x2, wp, ps, wo, gpost, gpre, layer, seq):
    m, d = x2.shape
    t = SEQ_TILE
    tiles = seq // t
    pw = u.shape[1]
    halo_blocks = t // POOL_HALO
    const = lambda i: (0, 0)
    return pl.pallas_call(
        functools.partial(_mix_out_kernel, tiles_per_seq=tiles),
        grid=(m // t,),
        in_specs=[
            pl.BlockSpec((t, o.shape[1]), lambda i: (i, 0)),
            pl.BlockSpec((t, pw), lambda i: (i, 0)),
            pl.BlockSpec((POOL_HALO, pw), lambda i: (jnp.maximum(i * halo_blocks - 1, 0), 0)),
            pl.BlockSpec((t, d), lambda i: (i, 0)),
            _layer(layer, wp.shape[1:], lambda i: (0, 0, 0), resident=True),
            _layer(layer, (1, pw), const, resident=True),
            _layer(layer, wo.shape[1:], const, resident=True),
            _layer(layer, (1, d), const, resident=True),
            _layer(layer, (1, d), const, resident=True),
        ],
        out_specs=[
            pl.BlockSpec((t, d), lambda i: (i, 0)),
            pl.BlockSpec((t, d), lambda i: (i, 0)),
        ],
        out_shape=[
            jax.ShapeDtypeStruct((m, d), F32),
            jax.ShapeDtypeStruct((m, d), BF16),
        ],
        compiler_params=_params(1),
        name="mix_out",
    )(o, u, u, x2, wp, ps, wo, gpost, gpre)


def _ffn_up_kernel(h_ref, wg_ref, wv_ref, cwg_ref, cwv_ref, cbg_ref, cbv_ref, a_ref,
                   carry_g_ref, carry_v_ref, w_ref, *, tiles_per_seq):
    st = pl.program_id(1) % tiles_per_seq

    @pl.when(pl.program_id(1) == 0)
    def _():
        w_ref[0] = wg_ref[...].astype(BF16)
        w_ref[1] = wv_ref[...].astype(BF16)

    @pl.when(st == 0)
    def _():
        carry_g_ref[...] = jnp.zeros_like(carry_g_ref)
        carry_v_ref[...] = jnp.zeros_like(carry_v_ref)

    def conv(up, prev, cw_ref, cb_ref):
        ext = jnp.concatenate([prev, up], axis=0)
        x1 = pltpu.roll(ext, 1, 0)[CONV_HALO:, :]
        x2 = pltpu.roll(ext, 2, 0)[CONV_HALO:, :]
        return cb_ref[...] + x2 * cw_ref[0:1, :] + x1 * cw_ref[1:2, :] + up * cw_ref[2:3, :]

    prev_g = carry_g_ref[...]
    prev_v = carry_v_ref[...]
    assert sum(FFN_ROW_CHUNKS) == h_ref.shape[0]
    r = 0
    for n in FFN_ROW_CHUNKS:
        h = h_ref[r:r + n, :]
        up_g = jnp.dot(h, w_ref[0], preferred_element_type=F32)
        up_v = jnp.dot(h, w_ref[1], preferred_element_type=F32)
        gate = conv(up_g, prev_g, cwg_ref, cbg_ref)
        val = conv(up_v, prev_v, cwv_ref, cbv_ref)
        a_ref[r:r + n, :] = (jax.nn.gelu(gate, approximate=True) * val).astype(BF16)
        prev_g = up_g[n - CONV_HALO:, :]
        prev_v = up_v[n - CONV_HALO:, :]
        r += n
    carry_g_ref[...] = prev_g
    carry_v_ref[...] = prev_v


def _ffn_up(h, w_up, conv_w, conv_b, layer, seq):
    m, d = h.shape
    d_ff = w_up.shape[2] // 2
    tm, tn = FFN_ROW_TILE, FFN_COL_TILE
    nb = d_ff // tn
    tiles = seq // tm
    return pl.pallas_call(
        functools.partial(_ffn_up_kernel, tiles_per_seq=tiles),
        grid=(nb, m // tm),
        in_specs=[
            pl.BlockSpec((tm, d), lambda n, i: (i, 0)),
            _layer(layer, (d, tn), lambda n, i: (0, n)),
            _layer(layer, (d, tn), lambda n, i: (0, n + nb)),
            _layer(layer, (CONV_WIDTH, tn), lambda n, i: (0, n)),
            _layer(layer, (CONV_WIDTH, tn), lambda n, i: (0, n + nb)),
            _layer(layer, (1, tn), lambda n, i: (0, n)),
            _layer(layer, (1, tn), lambda n, i: (0, n + nb)),
        ],
        out_specs=pl.BlockSpec((tm, tn), lambda n, i: (i, n)),
        out_shape=jax.ShapeDtypeStruct((m, d_ff), BF16),
        scratch_shapes=[
            pltpu.VMEM((CONV_HALO, tn), F32),
            pltpu.VMEM((CONV_HALO, tn), F32),
            pltpu.VMEM((2, d, tn), BF16),
        ],
        compiler_params=_params(2),
        name="ffn_up",
    )(h, w_up, w_up, conv_w, conv_w, conv_b, conv_b)


def _ffn_down_kernel(a_ref, w_ref, x_ref, g_ref, xo_ref):
    y = jnp.dot(a_ref[...], w_ref[...], preferred_element_type=F32)
    xo_ref[...] = x_ref[...] + y * _rms_scale(y) * g_ref[...]


def _ffn_down(a, w, x2, g, layer):
    m, d = x2.shape
    d_ff = a.shape[1]
    tm = DOWN_ROW_TILE
    return pl.pallas_call(
        _ffn_down_kernel,
        grid=(m // tm,),
        in_specs=[
            pl.BlockSpec((tm, d_ff), lambda i: (i, 0)),
            _layer(layer, (d_ff, d), lambda i: (0, 0), resident=True),
            pl.BlockSpec((tm, d), lambda i: (i, 0)),
            _layer(layer, (1, d), lambda i: (0, 0), resident=True),
        ],
        out_specs=pl.BlockSpec((tm, d), lambda i: (i, 0)),
        out_shape=jax.ShapeDtypeStruct((m, d), F32),
        compiler_params=_params(1),
        name="ffn_down",
    )(a, w, x2, g)


def kernel(x, g_mix_pre, w_in, lam_q1, lam_k1, lam_q2, lam_k2, g_head, w_pool, pool_scale,
           w_out, g_mix_post, g_ffn_pre, w_up, conv_w, conv_b, w_down, g_ffn_post):
    batch, seq, d = x.shape
    depth = w_in.shape[0]
    assert seq % FFN_ROW_TILE == 0 and seq % (SEQ_TILE * ATTN_Q_TILES) == 0
    assert g_head.shape[1] == N_HEADS * D_V
    assert (w_up.shape[2] // 2) % FFN_COL_TILE == 0

    w_in_b = w_in.astype(BF16)
    w_pool_b = w_pool.astype(BF16)
    w_out_b = w_out.astype(BF16)
    w_down_b = w_down.astype(BF16)
    rows = lambda p: p[:, None, :]
    g_mix_pre, g_mix_post, g_ffn_pre, g_ffn_post = map(rows, (g_mix_pre, g_mix_post, g_ffn_pre, g_ffn_post))
    lam_q1, lam_k1, lam_q2, lam_k2 = map(rows, (lam_q1, lam_k1, lam_q2, lam_k2))
    pool_scale, conv_b = rows(pool_scale), rows(conv_b)
    g_head = g_head.reshape(depth, N_HEADS, D_V, 1)

    x2 = x.reshape(batch * seq, d)
    for i in range(depth):
        lam_init = 0.8 - 0.6 * math.exp(-0.3 * i)
        qT, k, vT, u = _in_proj(x2, g_mix_pre, w_in_b, i, batch, seq)
        o = _attention(qT, k, vT, g_head, lam_q1, lam_k1, lam_q2, lam_k2, i, batch, seq, lam_init)
        x2, h = _mix_out(o, u, x2, w_pool_b, pool_scale, w_out_b, g_mix_post, g_ffn_pre, i, seq)
        a = _ffn_up(h, w_up, conv_w, conv_b, i, seq)
        x2 = _ffn_down(a, w_down_b, x2, g_ffn_post, i)
    return x2.reshape(batch, seq, d)
```

```python
import functools
import math

import jax
import jax.numpy as jnp
from jax import lax
from jax.experimental import pallas as pl
from jax.experimental.pallas import tpu as pltpu

N_HEADS = 8
D_V = 128
D_QK = 64
POOL_WINDOWS = (2, 4, 8, 16)
POOL_HALO = 16
CONV_WIDTH = 3
CONV_HALO = 8
EPS = 1e-6
LOG2E = math.log2(math.e)
NEG_BIG = -1e30

SEQ_TILE = 512
ATTN_STRIP = 256
ATTN_SUM_ROWS = 16
ATTN_Q_TILES = 4
MIX_ROW_CHUNK = 256
FFN_ROW_TILE = 2048
FFN_ROW_CHUNKS = (1024, 1024)
FFN_COL_TILE = 512
DOWN_ROW_TILE = 256
VMEM_LIMIT_BYTES = 56 * 1024 * 1024

F32 = jnp.float32
BF16 = jnp.bfloat16


def _params(n_axes):
    return pltpu.CompilerParams(
        dimension_semantics=("arbitrary",) * n_axes,
        vmem_limit_bytes=VMEM_LIMIT_BYTES,
    )


def _layer(layer, block_shape, index_map, resident=False):
    return pl.BlockSpec((None,) + tuple(block_shape), lambda *ids: (layer,) + tuple(index_map(*ids)),
                        pipeline_mode=pl.Buffered(1) if resident else None)


def _rms_scale(x):
    return lax.rsqrt(jnp.mean(x * x, axis=-1, keepdims=True) + EPS)


def _in_proj_kernel(x_ref, g_ref, w_ref, qT_ref, k_ref, vT_ref, pm_ref, hist_ref, *, q_scale, tiles_per_seq):
    x = x_ref[...]
    h = (x * _rms_scale(x) * g_ref[...]).astype(BF16)
    width = N_HEADS * D_V
    yq = jnp.dot(h, w_ref[:, 0:width], preferred_element_type=F32) * q_scale
    for hd in range(N_HEADS):
        qT_ref[0, hd, 0] = yq[:, hd * D_V:(hd + 1) * D_V].T.astype(BF16)
    yk = jnp.dot(h, w_ref[:, width:2 * width], preferred_element_type=F32)
    k_ref[...] = yk.astype(BF16)
    yv = jnp.dot(h, w_ref[:, 2 * width:3 * width], preferred_element_type=F32)
    for hd in range(N_HEADS):
        vT_ref[0, hd, 0] = yv[:, hd * D_V:(hd + 1) * D_V].T.astype(BF16)
    u = jnp.dot(h, w_ref[:, 3 * width:], preferred_element_type=F32)
    t = u.shape[0]
    st = pl.program_id(0) % tiles_per_seq

    @pl.when(st == 0)
    def _():
        hist_ref[...] = jnp.zeros_like(hist_ref)

    ext = jnp.concatenate([hist_ref[...], u], axis=0)
    pos = (st * t + lax.broadcasted_iota(jnp.int32, (t, 1), 0) + 1).astype(F32)
    group = u.shape[1] // len(POOL_WINDOWS)
    for g, w in enumerate(POOL_WINDOWS):
        cols = slice(g * group, (g + 1) * group)
        s = ext[:, cols]
        span = 1
        while span < w:
            s = s + pltpu.roll(s, span, 0)
            span *= 2
        pm_ref[:, cols] = (s[POOL_HALO:, :] / jnp.minimum(pos, float(w)) - u[:, cols]).astype(BF16)
    hist_ref[...] = u[t - POOL_HALO:, :]


def _in_proj(x2, g, w, layer, batch, seq):
    m, d = x2.shape
    t = SEQ_TILE
    tiles = seq // t
    width = N_HEADS * D_V
    q_scale = (D_QK ** -0.5) * LOG2E
    t_shape = (batch, N_HEADS, tiles, D_V, t)
    t_spec = pl.BlockSpec((1, N_HEADS, 1, D_V, t), lambda i: (i // tiles, 0, i % tiles, 0, 0))
    return pl.pallas_call(
        functools.partial(_in_proj_kernel, q_scale=q_scale, tiles_per_seq=tiles),
        grid=(m // t,),
        in_specs=[
            pl.BlockSpec((t, d), lambda i: (i, 0)),
            _layer(layer, (1, d), lambda i: (0, 0), resident=True),
            _layer(layer, (d, 4 * width), lambda i: (0, 0), resident=True),
        ],
        out_specs=[
            t_spec,
            pl.BlockSpec((t, width), lambda i: (i, 0)),
            t_spec,
            pl.BlockSpec((t, width), lambda i: (i, 0)),
        ],
        out_shape=[
            jax.ShapeDtypeStruct(t_shape, BF16),
            jax.ShapeDtypeStruct((m, width), BF16),
            jax.ShapeDtypeStruct(t_shape, BF16),
            jax.ShapeDtypeStruct((m, width), BF16),
        ],
        scratch_shapes=[pltpu.VMEM((POOL_HALO, width), F32)],
        compiler_params=_params(1),
        name="in_proj",
    )(x2, g, w)


def _split3(v):
    a = v.astype(BF16).astype(F32)
    b = (v - a).astype(BF16).astype(F32)
    c = (v - a - b).astype(BF16).astype(F32)
    return a, b, c


def _attn_kernel(lq1_ref, lk1_ref, lq2_ref, lk2_ref, gh_ref, qT_ref, k_ref, vT_ref, o_ref,
                 pf_ref, qf_ref, mask_ref, s_ref, acc_ref, stat_ref, *, lam_init):
    t = SEQ_TILE
    w = ATTN_STRIP
    nq = ATTN_Q_TILES
    tq = nq * t
    hd = pl.program_id(1)
    qi = pl.program_id(2)
    base = qi * nq

    def slope_row(n):
        return jnp.exp2(jnp.full((1, n), -8.0 / N_HEADS, F32) * (hd + 1).astype(F32)) * LOG2E

    slope = slope_row(tq)
    q_tile = lax.broadcasted_iota(jnp.int32, (1, tq), 1) // t

    @pl.when(qi == 0)
    def _():
        c = lax.broadcasted_iota(jnp.int32, (t, D_V), 0)
        lane = lax.broadcasted_iota(jnp.int32, (t, D_V), 1)
        c_hi = ((c >> 8) << 8).astype(F32)
        c_lo = (c & 255).astype(F32)
        pf = jnp.where(lane < 3, c_hi, jnp.where(lane < 6, c_lo, jnp.where(lane < 9, 1.0, 0.0)))
        pf_ref[...] = pf.astype(BF16)
        r = lax.broadcasted_iota(jnp.int32, (1, t), 1).astype(F32)
        sp = _split3(slope_row(t))
        gp = _split3(-slope_row(t) * r)
        row = lax.broadcasted_iota(jnp.int32, (D_V, t), 0)
        qf = jnp.zeros((D_V, t), F32)
        for i in range(3):
            qf = jnp.where((row == i) | (row == i + 3), sp[i], qf)
            qf = jnp.where(row == i + 6, gp[i], qf)
        qf_ref[...] = qf.astype(BF16)
        key = lax.broadcasted_iota(jnp.int32, (t, t), 0)
        qry = lax.broadcasted_iota(jnp.int32, (t, t), 1)
        mask_ref[...] = jnp.where(key > qry, NEG_BIG, 0.0)

    row = lax.broadcasted_iota(jnp.int32, (D_V, t), 0)
    qf = qf_ref[...]
    qw = ([], [])
    for b in range(nq):
        qT = qT_ref[0, 0, b]
        zero = jnp.zeros_like(qT)
        qw[0].append(jnp.concatenate([jnp.where(row < D_QK, qT, zero), qf], axis=0))
        qw[1].append(jnp.concatenate([jnp.where(row < D_QK, zero, qT), qf], axis=0))

    acc_ref[...] = jnp.zeros_like(acc_ref)
    stat_ref[0:2] = jnp.full((2, 1, tq), NEG_BIG, F32)

    def strips_from(lo):
        return [(b, h * w) for b in range(lo, nq) for h in range(t // w)]

    def visible_keys(b, c, stair):
        return min(t, c + w) if b == stair else t

    def logits(mp, j, stair):
        lo = 0 if stair is None else stair
        start = pl.multiple_of(j * t, t)
        kt = jnp.concatenate([k_ref[pl.ds(start, t), :], pf_ref[...]], axis=1)
        parts = []
        for b, c in strips_from(lo):
            keys = visible_keys(b, c, stair)
            sb = jnp.dot(kt[0:keys, :], qw[mp][b][:, c:c + w], preferred_element_type=F32)
            if b == stair:
                sb = sb + mask_ref[0:keys, c:c + w]
            s_ref[mp, 0:keys, b * t + c:b * t + c + w] = sb
            parts.append(jnp.max(sb, axis=0, keepdims=True))
        stat_ref[2 + mp, :, lo * t:tq] = jnp.concatenate(parts, axis=1)

    ones = jnp.ones((ATTN_SUM_ROWS, t), BF16)

    def values(mp, j, stair):
        lo = 0 if stair is None else stair
        live = slice(lo * t, tq)
        off = (slope * ((j - base - q_tile) * t).astype(F32))[:, live]
        vt = jnp.concatenate([vT_ref[0, 0, j], ones], axis=0)
        m_old = stat_ref[mp, :, live]
        m_new = jnp.maximum(m_old, stat_ref[2 + mp, :, live] + off)
        stat_ref[mp, :, live] = m_new
        mm = m_new - off
        alpha = jnp.exp2(m_old - m_new)
        for b, c in strips_from(lo):
            cols = slice(b * t + c, b * t + c + w)
            loc = slice((b - lo) * t + c, (b - lo) * t + c + w)
            keys = visible_keys(b, c, stair)
            e = jnp.exp2(s_ref[mp, 0:keys, cols] - mm[:, loc]).astype(BF16)
            pv = jnp.dot(vt[:, 0:keys], e, preferred_element_type=F32)
            acc_ref[mp, :, cols] = acc_ref[mp, :, cols] * alpha[:, loc] + pv

    def step(j, stair, stair_next):
        logits(1, j, stair)
        values(0, j, stair)
        logits(0, j + 1, stair_next)
        values(1, j, stair)

    @pl.when(qi == 0)
    def _():
        logits(0, 0, 0)

    @pl.when(qi > 0)
    def _():
        logits(0, 0, None)

    def body(p, carry):
        step(2 * p + 1, None, None)
        step(2 * p + 2, None, None)
        return carry

    @pl.when(qi > 0)
    def _():
        step(0, None, None)

    lax.fori_loop(0, (base - 2) // 2, body, 0)

    @pl.when(qi > 0)
    def _():
        step(base - 1, None, 0)

    for a in range(nq - 1):
        step(base + a, a, a + 1)
    last = base + nq - 1
    logits(1, last, nq - 1)
    values(0, last, nq - 1)
    values(1, last, nq - 1)

    lam = (jnp.exp(jnp.sum(lq1_ref[...] * lk1_ref[...], axis=-1, keepdims=True))
           - jnp.exp(jnp.sum(lq2_ref[...] * lk2_ref[...], axis=-1, keepdims=True))
           + lam_init)
    for b in range(nq):
        cols = slice(b * t, (b + 1) * t)
        l1 = acc_ref[0, D_V:D_V + 1, cols]
        l2 = acc_ref[1, D_V:D_V + 1, cols]
        o = acc_ref[0, 0:D_V, cols] * (1.0 / l1) - lam * (acc_ref[1, 0:D_V, cols] * (1.0 / l2))
        inv = lax.rsqrt(jnp.mean(o * o, axis=0, keepdims=True) + EPS)
        on = (o * inv) * gh_ref[0] * (1.0 - lam_init)
        o_ref[cols, :] = on.T.astype(BF16)


def _attention(qT, k, vT, gh, lq1, lk1, lq2, lk2, layer, batch, seq, lam_init):
    assert ATTN_Q_TILES % 2 == 0
    t = SEQ_TILE
    nq = ATTN_Q_TILES
    tq = nq * t
    tiles = seq // t
    blocks = seq // tq
    m, width = k.shape
    small = _layer(layer, (1, D_QK), lambda b, h, q: (0, 0), resident=True)
    return pl.pallas_call(
        functools.partial(_attn_kernel, lam_init=lam_init),
        grid=(batch, N_HEADS, blocks),
        in_specs=[
            small, small, small, small,
            _layer(layer, (1, D_V, 1), lambda b, h, q: (h, 0, 0)),
            pl.BlockSpec((1, 1, nq, D_V, t), lambda b, h, q: (b, h, q, 0, 0)),
            pl.BlockSpec((seq, D_V), lambda b, h, q: (b, h)),
            pl.BlockSpec((1, 1, tiles, D_V, t), lambda b, h, q: (b, h, 0, 0, 0)),
        ],
        out_specs=pl.BlockSpec((tq, D_V), lambda b, h, q: (b * blocks + q, h)),
        out_shape=jax.ShapeDtypeStruct((m, width), BF16),
        scratch_shapes=[
            pltpu.VMEM((t, D_V), BF16),
            pltpu.VMEM((D_V, t), BF16),
            pltpu.VMEM((t, t), F32),
            pltpu.VMEM((2, t, tq), F32),
            pltpu.VMEM((2, D_V + ATTN_SUM_ROWS, tq), F32),
            pltpu.VMEM((4, 1, tq), F32),
        ],
        compiler_params=_params(3),
        name="attn",
    )(lq1, lk1, lq2, lk2, gh, qT, k, vT)


def _mix_out_kernel(o_ref, pm_ref, x_ref, wp_ref, ps_ref, wo_ref, gpost_ref, gpre_ref, xo_ref, h_ref):
    t = SEQ_TILE
    n = MIX_ROW_CHUNK
    group = pm_ref.shape[1] // len(POOL_WINDOWS)
    attn_w = o_ref.shape[1]

    for r in range(0, t, n):
        mix = jnp.dot(o_ref[r:r + n, :], wo_ref[0:attn_w, :], preferred_element_type=F32)
        pooled = []
        for g in range(len(POOL_WINDOWS)):
            cols = slice(g * group, (g + 1) * group)
            pw = jnp.dot(pm_ref[r:r + n, cols], wp_ref[g], preferred_element_type=F32) * ps_ref[:, cols]
            pooled.append(pw.astype(BF16))
        mix = mix + jnp.dot(jnp.concatenate(pooled, axis=1), wo_ref[attn_w:, :], preferred_element_type=F32)
        xn = x_ref[r:r + n, :] + mix * _rms_scale(mix) * gpost_ref[...]
        xo_ref[r:r + n, :] = xn
        h_ref[r:r + n, :] = (xn * _rms_scale(xn) * gpre_ref[...]).astype(BF16)


def _mix_out(o, pm, x2, wp, ps, wo, gpost, gpre, layer):
    m, d = x2.shape
    t = SEQ_TILE
    pw = pm.shape[1]
    const = lambda i: (0, 0)
    return pl.pallas_call(
        _mix_out_kernel,
        grid=(m // t,),
        in_specs=[
            pl.BlockSpec((t, o.shape[1]), lambda i: (i, 0)),
            pl.BlockSpec((t, pw), lambda i: (i, 0)),
            pl.BlockSpec((t, d), lambda i: (i, 0)),
            _layer(layer, wp.shape[1:], lambda i: (0, 0, 0), resident=True),
            _layer(layer, (1, pw), const, resident=True),
            _layer(layer, wo.shape[1:], const, resident=True),
            _layer(layer, (1, d), const, resident=True),
            _layer(layer, (1, d), const, resident=True),
        ],
        out_specs=[
            pl.BlockSpec((t, d), lambda i: (i, 0)),
            pl.BlockSpec((t, d), lambda i: (i, 0)),
        ],
        out_shape=[
            jax.ShapeDtypeStruct((m, d), F32),
            jax.ShapeDtypeStruct((m, d), BF16),
        ],
        compiler_params=_params(1),
        name="mix_out",
    )(o, pm, x2, wp, ps, wo, gpost, gpre)


def _ffn_up_kernel(h_ref, wg_ref, wv_ref, cwg_ref, cwv_ref, cbg_ref, cbv_ref, a_ref,
                   carry_g_ref, carry_v_ref, w_ref, *, tiles_per_seq):
    st = pl.program_id(1) % tiles_per_seq

    @pl.when(pl.program_id(1) == 0)
    def _():
        w_ref[0] = wg_ref[...].astype(BF16)
        w_ref[1] = wv_ref[...].astype(BF16)

    @pl.when(st == 0)
    def _():
        carry_g_ref[...] = jnp.zeros_like(carry_g_ref)
        carry_v_ref[...] = jnp.zeros_like(carry_v_ref)

    def conv(up, prev, cw_ref, cb_ref):
        ext = jnp.concatenate([prev, up], axis=0)
        x1 = pltpu.roll(ext, 1, 0)[CONV_HALO:, :]
        x2 = pltpu.roll(ext, 2, 0)[CONV_HALO:, :]
        return cb_ref[...] + x2 * cw_ref[0:1, :] + x1 * cw_ref[1:2, :] + up * cw_ref[2:3, :]

    prev_g = carry_g_ref[...]
    prev_v = carry_v_ref[...]
    assert sum(FFN_ROW_CHUNKS) == h_ref.shape[0]
    r = 0
    for n in FFN_ROW_CHUNKS:
        h = h_ref[r:r + n, :]
        up_g = jnp.dot(h, w_ref[0], preferred_element_type=F32)
        up_v = jnp.dot(h, w_ref[1], preferred_element_type=F32)
        gate = conv(up_g, prev_g, cwg_ref, cbg_ref)
        val = conv(up_v, prev_v, cwv_ref, cbv_ref)
        a_ref[r:r + n, :] = (jax.nn.gelu(gate, approximate=True) * val).astype(BF16)
        prev_g = up_g[n - CONV_HALO:, :]
        prev_v = up_v[n - CONV_HALO:, :]
        r += n
    carry_g_ref[...] = prev_g
    carry_v_ref[...] = prev_v


def _ffn_up(h, w_up, conv_w, conv_b, layer, seq):
    m, d = h.shape
    d_ff = w_up.shape[2] // 2
    tm, tn = FFN_ROW_TILE, FFN_COL_TILE
    nb = d_ff // tn
    tiles = seq // tm
    return pl.pallas_call(
        functools.partial(_ffn_up_kernel, tiles_per_seq=tiles),
        grid=(nb, m // tm),
        in_specs=[
            pl.BlockSpec((tm, d), lambda n, i: (i, 0)),
            _layer(layer, (d, tn), lambda n, i: (0, n)),
            _layer(layer, (d, tn), lambda n, i: (0, n + nb)),
            _layer(layer, (CONV_WIDTH, tn), lambda n, i: (0, n)),
            _layer(layer, (CONV_WIDTH, tn), lambda n, i: (0, n + nb)),
            _layer(layer, (1, tn), lambda n, i: (0, n)),
            _layer(layer, (1, tn), lambda n, i: (0, n + nb)),
        ],
        out_specs=pl.BlockSpec((tm, tn), lambda n, i: (i, n)),
        out_shape=jax.ShapeDtypeStruct((m, d_ff), BF16),
        scratch_shapes=[
            pltpu.VMEM((CONV_HALO, tn), F32),
            pltpu.VMEM((CONV_HALO, tn), F32),
            pltpu.VMEM((2, d, tn), BF16),
        ],
        compiler_params=_params(2),
        name="ffn_up",
    )(h, w_up, w_up, conv_w, conv_w, conv_b, conv_b)


def _ffn_down_kernel(a_ref, w_ref, x_ref, g_ref, xo_ref):
    y = jnp.dot(a_ref[...], w_ref[...], preferred_element_type=F32)
    xo_ref[...] = x_ref[...] + y * _rms_scale(y) * g_ref[...]


def _ffn_down(a, w, x2, g, layer):
    m, d = x2.shape
    d_ff = a.shape[1]
    tm = DOWN_ROW_TILE
    return pl.pallas_call(
        _ffn_down_kernel,
        grid=(m // tm,),
        in_specs=[
            pl.BlockSpec((tm, d_ff), lambda i: (i, 0)),
            _layer(layer, (d_ff, d), lambda i: (0, 0), resident=True),
            pl.BlockSpec((tm, d), lambda i: (i, 0)),
            _layer(layer, (1, d), lambda i: (0, 0), resident=True),
        ],
        out_specs=pl.BlockSpec((tm, d), lambda i: (i, 0)),
        out_shape=jax.ShapeDtypeStruct((m, d), F32),
        compiler_params=_params(1),
        name="ffn_down",
    )(a, w, x2, g)


def kernel(x, g_mix_pre, w_in, lam_q1, lam_k1, lam_q2, lam_k2, g_head, w_pool, pool_scale,
           w_out, g_mix_post, g_ffn_pre, w_up, conv_w, conv_b, w_down, g_ffn_post):
    batch, seq, d = x.shape
    depth = w_in.shape[0]
    assert seq % FFN_ROW_TILE == 0 and seq % (SEQ_TILE * ATTN_Q_TILES) == 0
    assert g_head.shape[1] == N_HEADS * D_V
    assert (w_up.shape[2] // 2) % FFN_COL_TILE == 0

    w_in_b = w_in.astype(BF16)
    w_pool_b = w_pool.astype(BF16)
    w_out_b = w_out.astype(BF16)
    w_down_b = w_down.astype(BF16)
    rows = lambda p: p[:, None, :]
    g_mix_pre, g_mix_post, g_ffn_pre, g_ffn_post = map(rows, (g_mix_pre, g_mix_post, g_ffn_pre, g_ffn_post))
    lam_q1, lam_k1, lam_q2, lam_k2 = map(rows, (lam_q1, lam_k1, lam_q2, lam_k2))
    pool_scale, conv_b = rows(pool_scale), rows(conv_b)
    g_head = g_head.reshape(depth, N_HEADS, D_V, 1)

    x2 = x.reshape(batch * seq, d)
    for i in range(depth):
        lam_init = 0.8 - 0.6 * math.exp(-0.3 * i)
        qT, k, vT, pm = _in_proj(x2, g_mix_pre, w_in_b, i, batch, seq)
        o = _attention(qT, k, vT, g_head, lam_q1, lam_k1, lam_q2, lam_k2, i, batch, seq, lam_init)
        x2, h = _mix_out(o, pm, x2, w_pool_b, pool_scale, w_out_b, g_mix_post, g_ffn_pre, i)
        a = _ffn_up(h, w_up, conv_w, conv_b, i, seq)
        x2 = _ffn_down(a, w_down_b, x2, g_ffn_post, i)
    return x2.reshape(batch, seq, d)
```

```python
import functools
import math

import jax
import jax.numpy as jnp
from jax import lax
from jax.experimental import pallas as pl
from jax.experimental.pallas import tpu as pltpu

N_HEADS = 8
D_V = 128
D_QK = 64
POOL_WINDOWS = (2, 4, 8, 16)
POOL_HALO = 16
CONV_WIDTH = 3
CONV_HALO = 8
EPS = 1e-6
LOG2E = math.log2(math.e)
NEG_BIG = -1e30

SEQ_TILE = 512
ATTN_STRIP = 256
ATTN_SUM_ROWS = 16
ATTN_Q_TILES = 4
MIX_ROW_CHUNK = 256
FFN_ROW_TILE = 2048
FFN_ROW_CHUNKS = (1024, 1024)
FFN_COL_TILE = 512
DOWN_ROW_TILE = 256
VMEM_LIMIT_BYTES = 56 * 1024 * 1024

F32 = jnp.float32
BF16 = jnp.bfloat16


def _params(n_axes):
    return pltpu.CompilerParams(
        dimension_semantics=("arbitrary",) * n_axes,
        vmem_limit_bytes=VMEM_LIMIT_BYTES,
    )


def _layer(layer, block_shape, index_map, resident=False):
    return pl.BlockSpec((None,) + tuple(block_shape), lambda *ids: (layer,) + tuple(index_map(*ids)),
                        pipeline_mode=pl.Buffered(1) if resident else None)


def _rms_scale(x):
    return lax.rsqrt(jnp.mean(x * x, axis=-1, keepdims=True) + EPS)


def _in_proj_kernel(x_ref, g_ref, w_ref, qT_ref, k_ref, vT_ref, pm_ref, hist_ref, *, q_scale, tiles_per_seq):
    st = pl.program_id(0) % tiles_per_seq

    @pl.when(st == 0)
    def _():
        hist_ref[...] = jnp.zeros_like(hist_ref)

    x = x_ref[...]
    h = (x * _rms_scale(x) * g_ref[...]).astype(BF16)
    width = N_HEADS * D_V
    u = jnp.dot(h, w_ref[:, 3 * width:], preferred_element_type=F32)
    t = u.shape[0]
    ext =jnp.concatenate([hist_ref[...], u], axis=0)
    pos = (st * t + lax.broadcasted_iota(jnp.int32, (t, 1), 0) + 1).astype(F32)
    group = u.shape[1] // len(POOL_WINDOWS)
    for g, w in enumerate(POOL_WINDOWS):
        cols = slice(g * group, (g + 1) * group)
        s = ext[:, cols]
        span = 1
        while span < w:
            s = s + pltpu.roll(s, span, 0)
            span *= 2
        pm_ref[:, cols] = (s[POOL_HALO:, :] / jnp.minimum(pos, float(w)) - u[:, cols]).astype(BF16)
    hist_ref[...] = u[t - POOL_HALO:, :]

    yq = jnp.dot(h, w_ref[:, 0:width], preferred_element_type=F32) * q_scale
    for hd in range(N_HEADS):
        qT_ref[0, hd, 0] = yq[:, hd * D_V:(hd + 1) * D_V].T.astype(BF16)
    yk = jnp.dot(h, w_ref[:, width:2 * width], preferred_element_type=F32)
    k_ref[...] = yk.astype(BF16)
    yv = jnp.dot(h, w_ref[:, 2 * width:3 * width], preferred_element_type=F32)
    for hd in range(N_HEADS):
        vT_ref[0, hd, 0] = yv[:, hd * D_V:(hd + 1) * D_V].T.astype(BF16)


def _in_proj(x2, g, w, layer, batch, seq):
    m, d = x2.shape
    t = SEQ_TILE
    tiles = seq // t
    width = N_HEADS * D_V
    q_scale = (D_QK ** -0.5) * LOG2E
    t_shape = (batch, N_HEADS, tiles, D_V, t)
    t_spec = pl.BlockSpec((1, N_HEADS, 1, D_V, t), lambda i: (i // tiles, 0, i % tiles, 0, 0))
    return pl.pallas_call(
        functools.partial(_in_proj_kernel, q_scale=q_scale, tiles_per_seq=tiles),
        grid=(m // t,),
        in_specs=[
            pl.BlockSpec((t, d), lambda i: (i, 0)),
            _layer(layer, (1, d), lambda i: (0, 0), resident=True),
            _layer(layer, (d, 4 * width), lambda i: (0, 0), resident=True),
        ],
        out_specs=[
            t_spec,
            pl.BlockSpec((t, width), lambda i: (i, 0)),
            t_spec,
            pl.BlockSpec((t, width), lambda i: (i, 0)),
        ],
        out_shape=[
            jax.ShapeDtypeStruct(t_shape, BF16),
            jax.ShapeDtypeStruct((m, width), BF16),
            jax.ShapeDtypeStruct(t_shape, BF16),
            jax.ShapeDtypeStruct((m, width), BF16),
        ],
        scratch_shapes=[pltpu.VMEM((POOL_HALO, width), F32)],
        compiler_params=_params(1),
        name="in_proj",
    )(x2, g, w)


def _split3(v):
    a = v.astype(BF16).astype(F32)
    b = (v - a).astype(BF16).astype(F32)
    c = (v - a - b).astype(BF16).astype(F32)
    return a, b, c


def _attn_kernel(lq1_ref, lk1_ref, lq2_ref, lk2_ref, gh_ref, qT_ref, k_ref, vT_ref, o_ref,
                 pf_ref, qf_ref, mask_ref, s_ref, acc_ref, stat_ref, *, lam_init):
    t = SEQ_TILE
    w = ATTN_STRIP
    nq = ATTN_Q_TILES
    tq = nq * t
    hd = pl.program_id(1)
    qi = pl.program_id(2)
    base = qi * nq

    def slope_row(n):
        return jnp.exp2(jnp.full((1, n), -8.0 / N_HEADS, F32) * (hd + 1).astype(F32)) * LOG2E

    slope = slope_row(tq)
    q_tile = lax.broadcasted_iota(jnp.int32, (1, tq), 1) // t

    @pl.when(qi == 0)
    def _():
        c = lax.broadcasted_iota(jnp.int32, (t, D_V), 0)
        lane = lax.broadcasted_iota(jnp.int32, (t, D_V), 1)
        c_hi = ((c >> 8) << 8).astype(F32)
        c_lo = (c & 255).astype(F32)
        pf = jnp.where(lane < 3, c_hi, jnp.where(lane < 6, c_lo, jnp.where(lane < 9, 1.0, 0.0)))
        pf_ref[...] = pf.astype(BF16)
        r = lax.broadcasted_iota(jnp.int32, (1, t), 1).astype(F32)
        sp = _split3(slope_row(t))
        gp = _split3(-slope_row(t) * r)
        row = lax.broadcasted_iota(jnp.int32, (D_V, t), 0)
        qf = jnp.zeros((D_V, t), F32)
        for i in range(3):
            qf = jnp.where((row == i) | (row == i + 3), sp[i], qf)
            qf = jnp.where(row == i + 6, gp[i], qf)
        qf_ref[...] = qf.astype(BF16)
        key = lax.broadcasted_iota(jnp.int32, (t, t), 0)
        qry = lax.broadcasted_iota(jnp.int32, (t, t), 1)
        mask_ref[...] = jnp.where(key > qry, NEG_BIG, 0.0)

    row = lax.broadcasted_iota(jnp.int32, (D_V, t), 0)
    qf = qf_ref[...]
    qw = ([], [])
    for b in range(nq):
        qT = qT_ref[0, 0, b]
        zero = jnp.zeros_like(qT)
        qw[0].append(jnp.concatenate([jnp.where(row < D_QK, qT, zero), qf], axis=0))
        qw[1].append(jnp.concatenate([jnp.where(row < D_QK, zero, qT), qf], axis=0))

    acc_ref[...] = jnp.zeros_like(acc_ref)
    stat_ref[0:2] = jnp.full((2, 1, tq), NEG_BIG, F32)

    def strips_from(lo):
        return [(b, h * w) for b in range(lo, nq) for h in range(t // w)]

    def visible_keys(b, c, stair):
        return min(t, c + w) if b == stair else t

    def logits(mp, j, stair):
        lo = 0 if stair is None else stair
        start = pl.multiple_of(j * t, t)
        kt = jnp.concatenate([k_ref[pl.ds(start, t), :], pf_ref[...]], axis=1)
        parts = []
        for b, c in strips_from(lo):
            keys = visible_keys(b, c, stair)
            sb = jnp.dot(kt[0:keys, :], qw[mp][b][:, c:c + w], preferred_element_type=F32)
            if b == stair:
                sb = sb + mask_ref[0:keys, c:c + w]
            s_ref[mp, 0:keys, b * t + c:b * t + c + w] = sb
            parts.append(jnp.max(sb, axis=0, keepdims=True))
        stat_ref[2 + mp, :, lo * t:tq] = jnp.concatenate(parts, axis=1)

    ones = jnp.ones((ATTN_SUM_ROWS, t), BF16)

    def values(mp, j, stair):
        lo = 0 if stair is None else stair
        live = slice(lo * t, tq)
        off = (slope * ((j - base - q_tile) * t).astype(F32))[:, live]
        vt = jnp.concatenate([vT_ref[0, 0, j], ones], axis=0)
        m_old = stat_ref[mp, :, live]
        m_new = jnp.maximum(m_old, stat_ref[2 + mp, :, live] + off)
        stat_ref[mp, :, live] = m_new
        mm = m_new - off
        alpha = jnp.exp2(m_old - m_new)
        for b, c in strips_from(lo):
            cols = slice(b * t + c, b * t + c + w)
            loc = slice((b - lo) * t + c, (b - lo) * t + c + w)
            keys = visible_keys(b, c, stair)
            e = jnp.exp2(s_ref[mp, 0:keys, cols] - mm[:, loc]).astype(BF16)
            pv = jnp.dot(vt[:, 0:keys], e, preferred_element_type=F32)
            acc_ref[mp, :, cols] = acc_ref[mp, :, cols] * alpha[:, loc] + pv

    def step(j, stair, stair_next):
        logits(1, j, stair)
        values(0, j, stair)
        logits(0, j + 1, stair_next)
        values(1, j, stair)

    @pl.when(qi == 0)
    def _():
        logits(0, 0, 0)

    @pl.when(qi > 0)
    def _():
        logits(0, 0, None)

    def body(p, carry):
        step(2 * p + 1, None, None)
        step(2 * p + 2, None, None)
        return carry

    @pl.when(qi > 0)
    def _():
        step(0, None, None)

    lax.fori_loop(0, (base - 2) // 2, body, 0)

    @pl.when(qi > 0)
    def _():
        step(base - 1, None, 0)

    for a in range(nq - 1):
        step(base + a, a, a + 1)
    last = base + nq - 1
    logits(1, last, nq - 1)
    values(0, last, nq - 1)
    values(1, last, nq - 1)

    lam = (jnp.exp(jnp.sum(lq1_ref[...] * lk1_ref[...], axis=-1, keepdims=True))
           - jnp.exp(jnp.sum(lq2_ref[...] * lk2_ref[...], axis=-1, keepdims=True))
           + lam_init)
    for b in range(nq):
        cols = slice(b * t, (b + 1) * t)
        l1 = acc_ref[0, D_V:D_V + 1, cols]
        l2 = acc_ref[1, D_V:D_V + 1, cols]
        o = acc_ref[0, 0:D_V, cols] * (1.0 / l1) - lam * (acc_ref[1, 0:D_V, cols] * (1.0 / l2))
        inv = lax.rsqrt(jnp.mean(o * o, axis=0, keepdims=True) + EPS)
        on = (o * inv) * gh_ref[0] * (1.0 - lam_init)
        o_ref[cols, :] = on.T.astype(BF16)


def _attention(qT, k, vT, gh, lq1, lk1, lq2, lk2, layer, batch, seq, lam_init):
    assert ATTN_Q_TILES % 2 == 0
    t = SEQ_TILE
    nq = ATTN_Q_TILES
    tq = nq * t
    tiles = seq // t
    blocks = seq // tq
    m, width = k.shape
    small = _layer(layer, (1, D_QK), lambda b, h, q: (0, 0), resident=True)
    return pl.pallas_call(
        functools.partial(_attn_kernel, lam_init=lam_init),
        grid=(batch, N_HEADS, blocks),
        in_specs=[
            small, small, small, small,
            _layer(layer, (1, D_V, 1), lambda b, h, q: (h, 0, 0)),
            pl.BlockSpec((1, 1, nq, D_V, t), lambda b, h, q: (b, h, q, 0, 0)),
            pl.BlockSpec((seq, D_V), lambda b, h, q: (b, h)),
            pl.BlockSpec((1, 1, tiles, D_V, t), lambda b, h, q: (b, h, 0, 0, 0)),
        ],
        out_specs=pl.BlockSpec((tq, D_V), lambda b, h, q: (b * blocks + q, h)),
        out_shape=jax.ShapeDtypeStruct((m, width), BF16),
        scratch_shapes=[
            pltpu.VMEM((t, D_V), BF16),
            pltpu.VMEM((D_V, t), BF16),
            pltpu.VMEM((t, t), F32),
            pltpu.VMEM((2, t, tq), F32),
            pltpu.VMEM((2, D_V + ATTN_SUM_ROWS, tq), F32),
            pltpu.VMEM((4, 1, tq), F32),
        ],
        compiler_params=_params(3),
        name="attn",
    )(lq1, lk1, lq2, lk2, gh, qT, k, vT)


def _mix_out_kernel(o_ref, pm_ref, x_ref, wp_ref, ps_ref, wo_ref, gpost_ref, gpre_ref, xo_ref, h_ref):
    t = SEQ_TILE
    n = MIX_ROW_CHUNK
    group = pm_ref.shape[1] // len(POOL_WINDOWS)
    attn_w = o_ref.shape[1]

    for r in range(0, t, n):
        mix = jnp.dot(o_ref[r:r + n, :], wo_ref[0:attn_w, :], preferred_element_type=F32)
        pooled = []
        for g in range(len(POOL_WINDOWS)):
            cols = slice(g * group, (g + 1) * group)
            pw = jnp.dot(pm_ref[r:r + n, cols], wp_ref[g], preferred_element_type=F32) * ps_ref[:, cols]
            pooled.append(pw.astype(BF16))
        mix = mix + jnp.dot(jnp.concatenate(pooled, axis=1), wo_ref[attn_w:, :], preferred_element_type=F32)
        xn = x_ref[r:r + n, :] + mix * _rms_scale(mix) * gpost_ref[...]
        xo_ref[r:r + n, :] = xn
        h_ref[r:r + n, :] = (xn * _rms_scale(xn) * gpre_ref[...]).astype(BF16)


def _mix_out(o, pm, x2, wp, ps, wo, gpost, gpre, layer):
    m, d = x2.shape
    t = SEQ_TILE
    pw = pm.shape[1]
    const = lambda i: (0, 0)
    return pl.pallas_call(
        _mix_out_kernel,
        grid=(m // t,),
        in_specs=[
            pl.BlockSpec((t, o.shape[1]), lambda i: (i, 0)),
            pl.BlockSpec((t, pw), lambda i: (i, 0)),
            pl.BlockSpec((t, d), lambda i: (i, 0)),
            _layer(layer, wp.shape[1:], lambda i: (0, 0, 0), resident=True),
            _layer(layer, (1, pw), const, resident=True),
            _layer(layer, wo.shape[1:], const, resident=True),
            _layer(layer, (1, d), const, resident=True),
            _layer(layer, (1, d), const, resident=True),
        ],
        out_specs=[
            pl.BlockSpec((t, d), lambda i: (i, 0)),
            pl.BlockSpec((t, d), lambda i: (i, 0)),
        ],
        out_shape=[
            jax.ShapeDtypeStruct((m, d), F32),
            jax.ShapeDtypeStruct((m, d), BF16),
        ],
        compiler_params=_params(1),
        name="mix_out",
    )(o, pm, x2, wp, ps, wo, gpost, gpre)


def _ffn_up_kernel(h_ref, wg_ref, wv_ref, cwg_ref, cwv_ref, cbg_ref, cbv_ref, a_ref,
                   carry_g_ref, carry_v_ref, w_ref, *, tiles_per_seq):
    st = pl.program_id(1) % tiles_per_seq

    @pl.when(pl.program_id(1) == 0)
    def _():
        w_ref[0] = wg_ref[...].astype(BF16)
        w_ref[1] = wv_ref[...].astype(BF16)

    @pl.when(st == 0)
    def _():
        carry_g_ref[...] = jnp.zeros_like(carry_g_ref)
        carry_v_ref[...] = jnp.zeros_like(carry_v_ref)

    def conv(up, prev, cw_ref, cb_ref):
        ext = jnp.concatenate([prev, up], axis=0)
        x1 = pltpu.roll(ext, 1, 0)[CONV_HALO:, :]
        x2 = pltpu.roll(ext, 2, 0)[CONV_HALO:, :]
        return cb_ref[...] + x2 * cw_ref[0:1, :] + x1 * cw_ref[1:2, :] + up * cw_ref[2:3, :]

    prev_g = carry_g_ref[...]
    prev_v = carry_v_ref[...]
    assert sum(FFN_ROW_CHUNKS) == h_ref.shape[0]
    r = 0
    for n in FFN_ROW_CHUNKS:
        h = h_ref[r:r + n, :]
        up_g = jnp.dot(h, w_ref[0], preferred_element_type=F32)
        up_v = jnp.dot(h, w_ref[1], preferred_element_type=F32)
        gate = conv(up_g, prev_g, cwg_ref, cbg_ref)
        val = conv(up_v, prev_v, cwv_ref, cbv_ref)
        a_ref[r:r + n, :] = (jax.nn.gelu(gate, approximate=True) * val).astype(BF16)
        prev_g = up_g[n - CONV_HALO:, :]
        prev_v = up_v[n - CONV_HALO:, :]
        r += n
    carry_g_ref[...] = prev_g
    carry_v_ref[...] = prev_v


def _ffn_up(h, w_up, conv_w, conv_b, layer, seq):
    m, d = h.shape
    d_ff = w_up.shape[2] // 2
    tm, tn = FFN_ROW_TILE, FFN_COL_TILE
    nb = d_ff // tn
    tiles = seq // tm
    return pl.pallas_call(
        functools.partial(_ffn_up_kernel, tiles_per_seq=tiles),
        grid=(nb, m // tm),
        in_specs=[
            pl.BlockSpec((tm, d), lambda n, i: (i, 0)),
            _layer(layer, (d, tn), lambda n, i: (0, n)),
            _layer(layer, (d, tn), lambda n, i: (0, n + nb)),
            _layer(layer, (CONV_WIDTH, tn), lambda n, i: (0, n)),
            _layer(layer, (CONV_WIDTH, tn), lambda n, i: (0, n + nb)),
            _layer(layer, (1, tn), lambda n, i: (0, n)),
            _layer(layer, (1, tn), lambda n, i: (0, n + nb)),
        ],
        out_specs=pl.BlockSpec((tm, tn), lambda n, i: (i, n)),
        out_shape=jax.ShapeDtypeStruct((m, d_ff), BF16),
        scratch_shapes=[
            pltpu.VMEM((CONV_HALO, tn), F32),
            pltpu.VMEM((CONV_HALO, tn), F32),
            pltpu.VMEM((2, d, tn), BF16),
        ],
        compiler_params=_params(2),
        name="ffn_up",
    )(h, w_up, w_up, conv_w, conv_w, conv_b, conv_b)


def _ffn_down_kernel(a_ref, w_ref, x_ref, g_ref, xo_ref):
    y = jnp.dot(a_ref[...], w_ref[...], preferred_element_type=F32)
    xo_ref[...] = x_ref[...] + y * _rms_scale(y) * g_ref[...]


def _ffn_down(a, w, x2, g, layer):
    m, d = x2.shape
    d_ff = a.shape[1]
    tm = DOWN_ROW_TILE
    return pl.pallas_call(
        _ffn_down_kernel,
        grid=(m // tm,),
        in_specs=[
            pl.BlockSpec((tm, d_ff), lambda i: (i, 0)),
            _layer(layer, (d_ff, d), lambda i: (0, 0), resident=True),
            pl.BlockSpec((tm, d), lambda i: (i, 0)),
            _layer(layer, (1, d), lambda i: (0, 0), resident=True),
        ],
        out_specs=pl.BlockSpec((tm, d), lambda i: (i, 0)),
        out_shape=jax.ShapeDtypeStruct((m, d), F32),
        compiler_params=_params(1),
        name="ffn_down",
    )(a, w, x2, g)


def kernel(x, g_mix_pre, w_in, lam_q1, lam_k1, lam_q2, lam_k2, g_head, w_pool, pool_scale,
           w_out, g_mix_post, g_ffn_pre, w_up, conv_w, conv_b, w_down, g_ffn_post):
    batch, seq, d = x.shape
    depth = w_in.shape[0]
    assert seq % FFN_ROW_TILE == 0 and seq % (SEQ_TILE * ATTN_Q_TILES) == 0
    assert g_head.shape[1] == N_HEADS * D_V
    assert (w_up.shape[2] // 2) % FFN_COL_TILE == 0

    w_in_b = w_in.astype(BF16)
    w_pool_b = w_pool.astype(BF16)
    w_out_b = w_out.astype(BF16)
    w_down_b = w_down.astype(BF16)
    rows = lambda p: p[:, None, :]
    g_mix_pre, g_mix_post, g_ffn_pre, g_ffn_post = map(rows, (g_mix_pre, g_mix_post, g_ffn_pre, g_ffn_post))
    lam_q1, lam_k1, lam_q2, lam_k2 = map(rows, (lam_q1, lam_k1, lam_q2, lam_k2))
    pool_scale, conv_b = rows(pool_scale), rows(conv_b)
    g_head = g_head.reshape(depth, N_HEADS, D_V, 1)

    x2 = x.reshape(batch * seq, d)
    for i in range(depth):
        lam_init = 0.8 - 0.6 * math.exp(-0.3 * i)
        qT, k, vT, pm = _in_proj(x2, g_mix_pre, w_in_b, i, batch, seq)
        o = _attention(qT, k, vT, g_head, lam_q1, lam_k1, lam_q2, lam_k2, i, batch, seq, lam_init)
        x2, h = _mix_out(o, pm, x2, w_pool_b, pool_scale, w_out_b, g_mix_post, g_ffn_pre, i)
        a = _ffn_up(h, w_up, conv_w, conv_b, i, seq)
        x2 = _ffn_down(a, w_down_b, x2, g_ffn_post, i)
    return x2.reshape(batch, seq, d)
```
